```python
import jax, jax.numpy as jnp
from jax import lax
import numpy as np

D_MODEL = 1024
BATCH = 32
SEQ = 2048
DEPTH = 1

HEAD_DIM = 64
N_HEADS_A = 8
N_KV_A = 2
GROUP_A = N_HEADS_A // N_KV_A
N_HEADS_B = 8
WIDTH_A = N_HEADS_A * HEAD_DIM
KV_WIDTH_A = N_KV_A * HEAD_DIM
WIDTH_B = N_HEADS_B * HEAD_DIM
MIX_WIDTH = WIDTH_A + WIDTH_B
IN_COLS = WIDTH_A + 2 * KV_WIDTH_A + 3 * WIDTH_B
SPLITS = (WIDTH_A, WIDTH_A + KV_WIDTH_A, WIDTH_A + 2 * KV_WIDTH_A,
          WIDTH_A + 2 * KV_WIDTH_A + WIDTH_B, WIDTH_A + 2 * KV_WIDTH_A + 2 * WIDTH_B)
WINDOW = 128
BLOCK = 128
N_META = 16
PAD = BLOCK - N_META
D_FF = -(-8 * D_MODEL // (3 * 256)) * 256
ROPE_THETA = 10000.0
EPS = 1e-6

kernel_name = "hymba_swa_sink_stickbreaking_block"


def rmsnorm(x, g):
    xf = x.astype(jnp.float32)
    y = xf * lax.rsqrt(jnp.mean(xf * xf, axis=-1, keepdims=True) + EPS)
    return (y * g.astype(jnp.float32)).astype(x.dtype)


def rope(x, pos):
    half = HEAD_DIM // 2
    inv_freq = ROPE_THETA ** (-jnp.arange(half, dtype=jnp.float32) / half)
    ang = pos.astype(jnp.float32)[:, None] * inv_freq[None, :]
    cos = jnp.cos(ang)[None, :, None, :]
    sin = jnp.sin(ang)[None, :, None, :]
    xf = x.astype(jnp.float32)
    x1, x2 = xf[..., :half], xf[..., half:]
    return jnp.concatenate([x1 * cos - x2 * sin, x2 * cos + x1 * sin], axis=-1).astype(x.dtype)


def swa_sink_attention(q, k, v, sinks):
    B, P = q.shape[0], q.shape[1]
    nb = P // BLOCK
    qb = q.reshape(B, nb, BLOCK, N_KV_A, GROUP_A, HEAD_DIM)
    kb = k.reshape(B, nb, BLOCK, N_KV_A, HEAD_DIM)
    vb = v.reshape(B, nb, BLOCK, N_KV_A, HEAD_DIM)

    def with_context(t):
        meta = jnp.broadcast_to(t[:, :1, PAD:BLOCK], (B, nb, N_META, N_KV_A, HEAD_DIM))
        prev = jnp.concatenate([jnp.zeros_like(t[:, :1]), t[:, :-1]], axis=1)
        return jnp.concatenate([meta, prev, t], axis=2)

    kc, vc = with_context(kb), with_context(vb)
    blk = jnp.arange(nb)[:, None]
    ar = jnp.arange(BLOCK)[None, :]
    qpos = blk * BLOCK + ar
    kpos = jnp.concatenate([
        jnp.broadcast_to(jnp.arange(PAD, BLOCK)[None, :], (nb, N_META)),
        (blk - 1) * BLOCK + ar,
        blk * BLOCK + ar], axis=1)
    nk = kpos.shape[1]
    dist = qpos[:, :, None] - kpos[:, None, :]
    in_meta_seg = (jnp.arange(nk) < N_META)[None, None, :]
    band = (kpos[:, None, :] >= PAD) & (dist >= 0) & (dist < WINDOW)
    mask = jnp.where(in_meta_seg, dist >= WINDOW, band)

    s = jnp.einsum('bnqhgd,bnkhd->bnhgqk', qb, kc).astype(jnp.float32) * (HEAD_DIM ** -0.5)
    s = jnp.where(mask[None, :, None, None], s, -jnp.inf)
    sink = sinks.astype(jnp.float32).reshape(N_KV_A, GROUP_A)[None, None, :, :, None, None]
    m = jnp.maximum(jnp.max(s, axis=-1, keepdims=True), sink)
    p = jnp.exp(s - m)
    denom = jnp.sum(p, axis=-1, keepdims=True) + jnp.exp(sink - m)
    p = (p / denom).astype(v.dtype)
    o = jnp.einsum('bnhgqk,bnkhd->bnqhgd', p, vc)
    return o.reshape(B, P, WIDTH_A)


def stick_breaking_attention(q, k, v):
    B, P = q.shape[0], q.shape[1]
    nb = P // BLOCK
    qb = q.reshape(B, nb, BLOCK, N_HEADS_B, HEAD_DIM).transpose(1, 0, 3, 2, 4)
    kt = k.transpose(0, 2, 1, 3)
    vt = v.transpose(0, 2, 1, 3)
    kpos = jnp.arange(P)
    scale = HEAD_DIM ** -0.5

    def one_block(args):
        q_blk, i = args
        qpos = i * BLOCK + jnp.arange(BLOCK)
        mask = (kpos[None, :] >= PAD) & (kpos[None, :] < qpos[:, None])
        z = jnp.einsum('bhqd,bhkd->bhqk', q_blk, kt).astype(jnp.float32) * scale
        log_1m_beta = jnp.where(mask, -jax.nn.softplus(z), 0.0)
        later = lax.cumsum(log_1m_beta, axis=3, reverse=True) - log_1m_beta
        a = jnp.where(mask, jnp.exp(-jax.nn.softplus(-z) + later), 0.0)
        return jnp.einsum('bhqk,bhkd->bhqd', a.astype(v.dtype), vt)

    o = lax.map(one_block, (qb, jnp.arange(nb)))
    return o.transpose(1, 0, 3, 2, 4).reshape(B, P, WIDTH_B)


def setup_inputs(seed: int = 0) -> dict:
    key = jax.random.key(seed)
    ks = jax.random.split(key, 16)
    f32 = jnp.float32

    def gain(k, shape):
        return 1.0 + 0.02 * jax.random.normal(k, shape, f32)

    return {
        "x": jax.random.normal(ks[0], (BATCH, SEQ, D_MODEL), f32),
        "meta_tokens": jax.random.normal(ks[1], (N_META, D_MODEL), f32),
        "norm_mix": gain(ks[2], (DEPTH, D_MODEL)),
        "w_in": jax.random.normal(ks[3], (DEPTH, D_MODEL, IN_COLS), f32) * D_MODEL ** -0.5,
        "sinks": 0.5 * jax.random.normal(ks[4], (DEPTH, N_HEADS_A), f32),
        "norm_out_a": gain(ks[5], (DEPTH, WIDTH_A)),
        "norm_out_b": gain(ks[6], (DEPTH, WIDTH_B)),
        "w_out": jax.random.normal(ks[7], (DEPTH, MIX_WIDTH, D_MODEL), f32) * MIX_WIDTH ** -0.5,
        "norm_ffn": gain(ks[8], (DEPTH, D_MODEL)),
        "w_gate": jax.random.normal(ks[9], (DEPTH, D_MODEL, D_FF), f32) * D_MODEL ** -0.5,
        "w_up": jax.random.normal(ks[10], (DEPTH, D_MODEL, D_FF), f32) * D_MODEL ** -0.5,
        "w_down": jax.random.normal(ks[11], (DEPTH, D_FF, D_MODEL), f32) * D_FF ** -0.5,
        "norm_final": gain(ks[12], (D_MODEL,)),
    }


def reference(x, meta_tokens, norm_mix, w_in, sinks, norm_out_a, norm_out_b, w_out,
              norm_ffn, w_gate, w_up, w_down, norm_final):
    B = x.shape[0]
    pad = jnp.zeros((B, PAD, D_MODEL), x.dtype)
    meta = jnp.broadcast_to(meta_tokens.astype(x.dtype)[None], (B, N_META, D_MODEL))
    h = jnp.concatenate([pad, meta, x], axis=1)
    P = h.shape[1]
    pos = jnp.arange(P) - PAD

    for l in range(DEPTH):
        u = rmsnorm(h, norm_mix[l])
        proj = u @ w_in[l]
        qa, ka, va, qb, kb, vb = jnp.split(proj, SPLITS, axis=-1)
        qa = rope(qa.reshape(B, P, N_HEADS_A, HEAD_DIM), pos)
        ka = rope(ka.reshape(B, P, N_KV_A, HEAD_DIM), pos)
        va = va.reshape(B, P, N_KV_A, HEAD_DIM)
        oa = swa_sink_attention(qa, ka, va, sinks[l])
        ob = stick_breaking_attention(
            qb.reshape(B, P, N_HEADS_B, HEAD_DIM),
            kb.reshape(B, P, N_HEADS_B, HEAD_DIM),
            vb.reshape(B, P, N_HEADS_B, HEAD_DIM))
        mixed = jnp.concatenate([rmsnorm(oa, norm_out_a[l]), rmsnorm(ob, norm_out_b[l])], axis=-1)
        h = h + mixed @ w_out[l]
        u = rmsnorm(h, norm_ffn[l])
        h = h + (jax.nn.silu(u @ w_gate[l]) * (u @ w_up[l])) @ w_down[l]

    return rmsnorm(h, norm_final)[:, BLOCK:]
```

```python
import functools

import jax
import jax.numpy as jnp
from jax import lax
from jax.experimental import pallas as pl
from jax.experimental.pallas import tpu as pltpu

HEAD_DIM = 64
N_HEADS_A = 8
N_KV_A = 2
N_HEADS_B = 8
WIDTH_A = N_HEADS_A * HEAD_DIM
KV_WIDTH_A = N_KV_A * HEAD_DIM
WIDTH_B = N_HEADS_B * HEAD_DIM
BLOCK = 128
N_META = 16
PAD = BLOCK - N_META
ROPE_THETA = 10000.0
EPS = 1e-6
LANES = 128
N_PAIRS_A = WIDTH_A // LANES
N_PAIRS_B = WIDTH_B // LANES
KV_DUP_A = 2 * KV_WIDTH_A

ROW_TILE_PROJ = 512
ROW_TILE_FFN = 512
VMEM_LIMIT = 56 * 1024 * 1024

F32 = jnp.float32
BF16 = jnp.bfloat16

_C_QA = 0
_C_KA = _C_QA + WIDTH_A
_C_VA = _C_KA + KV_DUP_A
_C_QB = _C_VA + KV_DUP_A
_C_KB = _C_QB + WIDTH_B
_C_VB = _C_KB + WIDTH_B
_C_END = _C_VB + WIDTH_B


def _rms(xf, g):
    ms = jnp.mean(xf * xf, axis=-1, keepdims=True)
    return xf * lax.rsqrt(ms + EPS) * g


def _dot(a, b):
    return jnp.dot(a, b, preferred_element_type=F32)


def _dot_nt(a, b):
    return lax.dot_general(a, b, (((1,), (1,)), ((), ())), preferred_element_type=F32)


def _in_proj_kernel(x_ref, g_ref, w_ref, cos_ref, sin_ref,
                    qa_ref, ka_ref, valo_ref, vahi_ref, qb_ref, kb_ref, vblo_ref, vbhi_ref):
    tm = x_ref.shape[0]
    u = _rms(x_ref[...], g_ref[...]).astype(BF16)
    proj = _dot(u, w_ref[...])
    cos = cos_ref[...]
    sin = sin_ref[...]
    lane = lax.broadcasted_iota(jnp.int32, (tm, LANES), 1)
    first_half = (lane & (HEAD_DIM // 2)) == 0
    lower_head = lane < HEAD_DIM
    scale = HEAD_DIM ** -0.5

    def rope(t):
        rot = jnp.where(first_half,
                        pltpu.roll(t, LANES - HEAD_DIM // 2, 1),
                        pltpu.roll(t, HEAD_DIM // 2, 1))
        return t * cos + rot * sin

    def group(c0, j):
        return proj[:, c0 + j * LANES:c0 + (j + 1) * LANES]

    for j in range(N_PAIRS_A):
        sl = slice(j * LANES, (j + 1) * LANES)
        qa_ref[:, sl] = (rope(group(_C_QA, j)) * scale).astype(BF16)
    for j in range(KV_DUP_A // LANES):
        sl = slice(j * LANES, (j + 1) * LANES)
        ka_ref[:, sl] = rope(group(_C_KA, j)).astype(BF16)
        v = group(_C_VA, j)
        valo_ref[:, sl] = jnp.where(lower_head, v, 0.0).astype(BF16)
        vahi_ref[:, sl] = jnp.where(lower_head, 0.0, v).astype(BF16)
    for j in range(N_PAIRS_B):
        sl = slice(j * LANES, (j + 1) * LANES)
        qb_ref[:, sl] = (group(_C_QB, j) * scale).astype(BF16)
        kb_ref[:, sl] = group(_C_KB, j).astype(BF16)
        v = group(_C_VB, j)
        vblo_ref[:, sl] = jnp.where(lower_head, v, 0.0).astype(BF16)
        vbhi_ref[:, sl] = jnp.where(lower_head, 0.0, v).astype(BF16)


def _in_proj(x2d, gain, w_cat, cos_t, sin_t, row_tile):
    n, d = x2d.shape
    table_tiles = cos_t.shape[0] // row_tile
    widths = (WIDTH_A, KV_DUP_A, KV_DUP_A, KV_DUP_A, WIDTH_B, WIDTH_B, WIDTH_B, WIDTH_B)
    row_spec = lambda w: pl.BlockSpec((row_tile, w), lambda i: (i, 0))
    const_spec = lambda shape: pl.BlockSpec(shape, lambda i: (0, 0), pipeline_mode=pl.Buffered(1))
    table_spec = pl.BlockSpec((row_tile, LANES), lambda i: (i % table_tiles, 0))
    return pl.pallas_call(
        _in_proj_kernel,
        grid=(n // row_tile,),
        in_specs=[row_spec(d), const_spec((1, d)), const_spec(w_cat.shape), table_spec, table_spec],
        out_specs=[row_spec(w) for w in widths],
        out_shape=[jax.ShapeDtypeStruct((n, w), BF16) for w in widths],
        compiler_params=pltpu.CompilerParams(
            dimension_semantics=("arbitrary",), vmem_limit_bytes=VMEM_LIMIT),
        name="in_proj",
    )(x2d, gain, w_cat, cos_t, sin_t)


def _stack_pair(q2):
    qf = q2.astype(F32)
    lower = lax.broadcasted_iota(jnp.int32, qf.shape, 1) < HEAD_DIM
    return jnp.concatenate([jnp.where(lower, qf, 0.0), jnp.where(lower, 0.0, qf)], axis=0).astype(BF16)


def _unstack(p):
    return jnp.concatenate([p[:BLOCK], p[BLOCK:]], axis=1)


def _attention_kernel(sinks_ref, qa_ref, ka_ref, valo_ref, vahi_ref, qb_ref, kb_ref, vblo_ref, vbhi_ref,
                      mka_ref, mvalo_ref, mvahi_ref, mkb_ref, mvblo_ref, mvbhi_ref,
                      ga_ref, gb_ref, tri_ref, out_ref, qs_ref, carry_ref, oacc_ref):
    qi = pl.program_id(1)
    cur0 = pl.multiple_of(qi * BLOCK, BLOCK)
    prev0 = pl.multiple_of(jnp.maximum(qi - 1, 0) * BLOCK, BLOCK)

    row = lax.broadcasted_iota(jnp.int32, (2 * BLOCK, LANES), 0)
    qq = row & (BLOCK - 1)
    kk = lax.broadcasted_iota(jnp.int32, (2 * BLOCK, LANES), 1)
    upper_rows = lax.broadcasted_iota(jnp.int32, (2 * BLOCK, 1), 0) >= BLOCK
    lower_lanes = lax.broadcasted_iota(jnp.int32, (BLOCK, LANES), 1) < HEAD_DIM
    is_meta = kk >= PAD
    neg_inf = -jnp.inf

    in_cur = kk <= qq
    has_prev = qi >= 1
    oa = []
    for g in range(N_PAIRS_A):
        c = (2 * g) // (N_HEADS_A // N_KV_A)
        csl = slice(c * LANES, (c + 1) * LANES)
        qs = _stack_pair(qa_ref[0, :, g * LANES:(g + 1) * LANES])
        s_cur = _dot_nt(qs, ka_ref[0, pl.ds(cur0, BLOCK), csl])
        s_prev = _dot_nt(qs, ka_ref[0, pl.ds(prev0, BLOCK), csl])
        s_meta = _dot_nt(qs, mka_ref[:, csl])
        s = jnp.where(in_cur, s_cur, jnp.where(has_prev, s_prev, neg_inf))
        sm = jnp.where(is_meta, s_meta, neg_inf)
        sink = jnp.where(upper_rows, sinks_ref[2 * g + 1], sinks_ref[2 * g])
        m = jnp.maximum(jnp.maximum(jnp.max(s, axis=-1, keepdims=True),
                                    jnp.max(sm, axis=-1, keepdims=True)), sink)
        p = jnp.exp(s - m)
        pm = jnp.exp(sm - m)
        den = jnp.sum(p, axis=-1, keepdims=True) + jnp.sum(pm, axis=-1, keepdims=True) + jnp.exp(sink - m)
        inv = 1.0 / den
        p_cur = jnp.where(in_cur, p, 0.0).astype(BF16)
        p_prev = jnp.where(in_cur, 0.0, p).astype(BF16)
        pcat = jnp.concatenate([_unstack(p_prev), _unstack(p_cur), _unstack(pm.astype(BF16))], axis=1)
        vcat = jnp.concatenate([valo_ref[0, pl.ds(prev0, BLOCK), csl], vahi_ref[0, pl.ds(prev0, BLOCK), csl],
                                valo_ref[0, pl.ds(cur0, BLOCK), csl], vahi_ref[0, pl.ds(cur0, BLOCK), csl],
                                mvalo_ref[:, csl], mvahi_ref[:, csl]], axis=0)
        o2 = _dot(pcat, vcat)
        oa.append(o2 * jnp.where(lower_lanes, inv[:BLOCK], inv[BLOCK:]))
    oa = jnp.concatenate(oa, axis=1)
    out_ref[0, :, :WIDTH_A] = _rms(oa, ga_ref[...]).astype(out_ref.dtype)

    for g in range(N_PAIRS_B):
        qs_ref[g] = _stack_pair(qb_ref[0, :, g * LANES:(g + 1) * LANES])
    carry_ref[...] = jnp.zeros_like(carry_ref)
    oacc_ref[...] = jnp.zeros_like(oacc_ref)
    tri = tri_ref[...]

    def b_block(g, k_blk, vlo_blk, vhi_blk, valid):
        z = _dot_nt(qs_ref[g], k_blk)
        sp = jnp.maximum(z, 0.0) + jnp.log(1.0 + jnp.exp(-jnp.abs(z)))
        log_beta = z - sp
        if valid is not None:
            sp = jnp.where(valid, sp, 0.0)
        hi = sp.astype(BF16)
        lo = (sp - hi.astype(F32)).astype(BF16)
        r = _dot(jnp.concatenate([hi, lo], axis=1), tri)
        carry = carry_ref[g]
        a = jnp.exp(log_beta + r[:, :LANES] + carry)
        if valid is not None:
            a = jnp.where(valid, a, 0.0)
        carry_ref[g] = carry + r[:, LANES:]
        acat = _unstack(a.astype(BF16))
        vcat = jnp.concatenate([vlo_blk, vhi_blk], axis=0)
        gsl = slice(g * LANES, (g + 1) * LANES)
        oacc_ref[:, gsl] += _dot(acat, vcat)

    def real_block(r0, valid):
        for g in range(N_PAIRS_B):
            gsl = slice(g * LANES, (g + 1) * LANES)
            b_block(g, kb_ref[0, pl.ds(r0, BLOCK), gsl], vblo_ref[0, pl.ds(r0, BLOCK), gsl],
                    vbhi_ref[0, pl.ds(r0, BLOCK), gsl], valid)

    real_block(cur0, kk < qq)

    def body(j, c):
        real_block(pl.multiple_of((qi - 1 - j) * BLOCK, BLOCK), None)
        return c

    lax.fori_loop(0, qi, body, 0)
    for g in range(N_PAIRS_B):
        gsl = slice(g * LANES, (g + 1) * LANES)
        b_block(g, mkb_ref[:, gsl], mvblo_ref[:, gsl], mvbhi_ref[:, gsl], is_meta)
    out_ref[0, :, WIDTH_A:] = _rms(oacc_ref[...], gb_ref[...]).astype(out_ref.dtype)


def _attention(sinks, real, meta, gain_a, gain_b, tri, batch, seq):
    nq = seq // BLOCK
    qa, ka, valo, vahi, qb, kb, vblo, vbhi = [t.reshape(batch, seq, t.shape[-1]) for t in real]
    _, mka, mvalo, mvahi, _, mkb, mvblo, mvbhi = meta
    qspec = lambda w: pl.BlockSpec((1, BLOCK, w), lambda b, i: (b, i, 0))
    seqspec = lambda w: pl.BlockSpec((1, seq, w), lambda b, i: (b, 0, 0))
    const = lambda a: pl.BlockSpec(a.shape, lambda b, i: (0,) * a.ndim)
    mix = WIDTH_A + WIDTH_B
    return pl.pallas_call(
        _attention_kernel,
        grid=(batch, nq),
        in_specs=[pl.BlockSpec(memory_space=pltpu.SMEM),
                  qspec(WIDTH_A), seqspec(KV_DUP_A), seqspec(KV_DUP_A), seqspec(KV_DUP_A),
                  qspec(WIDTH_B), seqspec(WIDTH_B), seqspec(WIDTH_B), seqspec(WIDTH_B),
                  const(mka), const(mvalo), const(mvahi), const(mkb), const(mvblo), const(mvbhi),
                  const(gain_a), const(gain_b), const(tri)],
        out_specs=pl.BlockSpec((1, BLOCK, mix), lambda b, i: (b, i, 0)),
        out_shape=jax.ShapeDtypeStruct((batch, seq, mix), BF16),
        scratch_shapes=[pltpu.VMEM((N_PAIRS_B, 2 * BLOCK, LANES), BF16),
                        pltpu.VMEM((N_PAIRS_B, 2 * BLOCK, LANES), F32),
                        pltpu.VMEM((BLOCK, WIDTH_B), F32)],
        compiler_params=pltpu.CompilerParams(
            dimension_semantics=("arbitrary", "arbitrary"), vmem_limit_bytes=VMEM_LIMIT),
        name="attention",
    )(sinks, qa, ka, valo, vahi, qb, kb, vblo, vbhi, mka, mvalo, mvahi, mkb, mvblo, mvbhi,
      gain_a, gain_b, tri)


def _out_ffn_kernel(ff_chunks, mixed_ref, x_ref, wo_ref, gf_ref, wg_ref, wu_ref, wd_ref, gl_ref, out_ref):
    h = x_ref[...] + _dot(mixed_ref[...], wo_ref[...])
    u = _rms(h, gf_ref[...]).astype(BF16)
    y = h
    for c0, c1 in ff_chunks:
        gate = _dot(u, wg_ref[:, c0:c1])
        up = _dot(u, wu_ref[:, c0:c1])
        act = (gate * jax.nn.sigmoid(gate) * up).astype(BF16)
        y = y + _dot(act, wd_ref[c0:c1, :])
    out_ref[...] = _rms(y, gl_ref[...])


def _ff_chunks(d_ff, n_chunks, col_tile):
    tiles = -(-d_ff // col_tile)
    bounds = [min(d_ff, (tiles * k // n_chunks) * col_tile) for k in range(n_chunks + 1)]
    return tuple((bounds[k], bounds[k + 1]) for k in range(n_chunks))


def _out_ffn(mixed2d, x2d, w_out, g_ffn, w_gate, w_up, w_down, g_final, row_tile):
    n, d = x2d.shape
    d_ff = w_gate.shape[1]
    row_spec = lambda w: pl.BlockSpec((row_tile, w), lambda i: (i, 0))
    const = lambda a: pl.BlockSpec(a.shape, lambda i: (0, 0), pipeline_mode=pl.Buffered(1))
    mxu_cols = 256
    return pl.pallas_call(
        functools.partial(_out_ffn_kernel, _ff_chunks(d_ff, 2, mxu_cols)),
        grid=(n // row_tile,),
        in_specs=[row_spec(mixed2d.shape[1]), row_spec(d), const(w_out), const(g_ffn),
                  const(w_gate), const(w_up), const(w_down), const(g_final)],
        out_specs=row_spec(d),
        out_shape=jax.ShapeDtypeStruct((n, d), x2d.dtype),
        compiler_params=pltpu.CompilerParams(
            dimension_semantics=("arbitrary",), vmem_limit_bytes=VMEM_LIMIT),
        name="out_ffn",
    )(mixed2d, x2d, w_out, g_ffn, w_gate, w_up, w_down, g_final)


def _rope_tables(pos):
    half = HEAD_DIM // 2
    inv_freq = ROPE_THETA ** (-jnp.arange(half, dtype=F32) / half)
    ang = pos.astype(F32)[:, None] * inv_freq[None, :]
    cos, sin = jnp.cos(ang), jnp.sin(ang)
    reps = LANES // HEAD_DIM
    return (jnp.tile(jnp.concatenate([cos, cos], axis=1), (1, reps)),
            jnp.tile(jnp.concatenate([-sin, sin], axis=1), (1, reps)))


def _rearranged_w_in(w):
    o_ka = WIDTH_A
    o_va = o_ka + KV_WIDTH_A
    o_qb = o_va + KV_WIDTH_A
    dup = lambda t: jnp.repeat(t.reshape(t.shape[0], N_KV_A, 1, HEAD_DIM), 2, axis=2).reshape(t.shape[0], KV_DUP_A)
    return jnp.concatenate([w[:, :o_ka], dup(w[:, o_ka:o_va]), dup(w[:, o_va:o_qb]), w[:, o_qb:]],
                           axis=1).astype(BF16)


def kernel(x, meta_tokens, norm_mix, w_in, sinks, norm_out_a, norm_out_b, w_out,
           norm_ffn, w_gate, w_up, w_down, norm_final):
    batch, seq, d = x.shape
    assert norm_mix.shape[0] == 1, "single-layer block: meta rows are not carried past the mixer"
    assert seq % ROW_TILE_PROJ == 0 and (batch * seq) % ROW_TILE_FFN == 0 and seq % BLOCK == 0
    x2d = x.reshape(batch * seq, d)
    w_cat = _rearranged_w_in(w_in[0])
    assert w_cat.shape[1] == _C_END
    g_mix = norm_mix[0].reshape(1, d)

    cos_r, sin_r = _rope_tables(jnp.arange(seq) + N_META)
    cos_m, sin_m = _rope_tables(jnp.arange(BLOCK) - PAD)
    meta_block = jnp.concatenate([jnp.zeros((PAD, d), x.dtype), meta_tokens.astype(x.dtype)], axis=0)

    real = _in_proj(x2d, g_mix, w_cat, cos_r, sin_r, ROW_TILE_PROJ)
    meta = _in_proj(meta_block, g_mix, w_cat, cos_m, sin_m, BLOCK)

    j = jnp.arange(2 * BLOCK)[:, None] % BLOCK
    s = jnp.arange(2 * BLOCK)[None, :]
    tri = jnp.where((s >= BLOCK) | (j > s), -1.0, 0.0).astype(BF16)

    mixed = _attention(sinks[0], real, meta, norm_out_a[0].reshape(1, WIDTH_A),
                       norm_out_b[0].reshape(1, WIDTH_B), tri, batch, seq)

    out = _out_ffn(mixed.reshape(batch * seq, WIDTH_A + WIDTH_B), x2d, w_out[0].astype(BF16),
                   norm_ffn[0].reshape(1, d), w_gate[0].astype(BF16), w_up[0].astype(BF16),
                   w_down[0].astype(BF16), norm_final.reshape(1, d), ROW_TILE_FFN)
    return out.reshape(batch, seq, d)
```

```python
import functools

import jax
import jax.numpy as jnp
from jax import lax
from jax.experimental import pallas as pl
from jax.experimental.pallas import tpu as pltpu

HEAD_DIM = 64
N_HEADS_A = 8
N_KV_A = 2
N_HEADS_B = 8
WIDTH_A = N_HEADS_A * HEAD_DIM
KV_WIDTH_A = N_KV_A * HEAD_DIM
WIDTH_B = N_HEADS_B * HEAD_DIM
BLOCK = 128
N_META = 16
PAD = BLOCK - N_META
ROPE_THETA = 10000.0
EPS = 1e-6
LANES = 128
N_PAIRS_A = WIDTH_A // LANES
N_PAIRS_B = WIDTH_B // LANES
KV_DUP_A = 2 * KV_WIDTH_A

Q_TILE = 2 * BLOCK
ROW_TILE_PROJ = 512
ROW_TILE_FFN = 512
VMEM_LIMIT = 56 * 1024 * 1024

F32 = jnp.float32
BF16 = jnp.bfloat16
LOG2_E = 1.4426950408889634

_C_QA = 0
_C_KA = _C_QA + WIDTH_A
_C_VA = _C_KA + KV_DUP_A
_C_QB = _C_VA + KV_DUP_A
_C_VB = _C_QB + WIDTH_B
_C_END = _C_VB + WIDTH_B


def _rms(xf, g):
    ms = jnp.mean(xf * xf, axis=-1, keepdims=True)
    return xf * lax.rsqrt(ms + EPS) * g


def _dot(a, b):
    return jnp.dot(a, b, preferred_element_type=F32)


def _dot_nt(a, b):
    return lax.dot_general(a, b, (((1,), (1,)), ((), ())), preferred_element_type=F32)


def _in_proj_kernel(x_ref, g_ref, w_ref, wkt_ref, cos_ref, sin_ref,
                    qa_ref, ka_ref, valo_ref, vahi_ref, qb_ref, ktlo_ref, kthi_ref, vblo_ref, vbhi_ref):
    tm = x_ref.shape[0]
    u = _rms(x_ref[...], g_ref[...]).astype(BF16)
    proj = _dot(u, w_ref[...])
    cos = cos_ref[...]
    sin = sin_ref[...]
    lane = lax.broadcasted_iota(jnp.int32, (tm, LANES), 1)
    first_half = (lane & (HEAD_DIM // 2)) == 0
    lower_head = lane < HEAD_DIM
    scale = HEAD_DIM ** -0.5

    def rope(t):
        rot = jnp.where(first_half,
                        pltpu.roll(t, LANES - HEAD_DIM // 2, 1),
                        pltpu.roll(t, HEAD_DIM // 2, 1))
        return t * cos + rot * sin

    def group(c0, j):
        return proj[:, c0 + j * LANES:c0 + (j + 1) * LANES]

    for j in range(N_PAIRS_A):
        sl = slice(j * LANES, (j + 1) * LANES)
        qa_ref[:, sl] = (rope(group(_C_QA, j)) * scale).astype(BF16)
    for j in range(KV_DUP_A // LANES):
        sl = slice(j * LANES, (j + 1) * LANES)
        ka_ref[:, sl] = rope(group(_C_KA, j)).astype(BF16)
        v = group(_C_VA, j)
        valo_ref[:, sl] = jnp.where(lower_head, v, 0.0).astype(BF16)
        vahi_ref[:, sl] = jnp.where(lower_head, 0.0, v).astype(BF16)
    for j in range(N_PAIRS_B):
        sl = slice(j * LANES, (j + 1) * LANES)
        qb_ref[:, sl] = (group(_C_QB, j) * (scale * LOG2_E)).astype(BF16)
        v = group(_C_VB, j)
        vblo_ref[:, sl] = jnp.where(lower_head, v, 0.0).astype(BF16)
        vbhi_ref[:, sl] = jnp.where(lower_head, 0.0, v).astype(BF16)

    kt = _dot_nt(wkt_ref[...], u)
    lower_feat = (lax.broadcasted_iota(jnp.int32, kt.shape, 0) & HEAD_DIM) == 0
    kt_lo = jnp.where(lower_feat, kt, 0.0).astype(BF16)
    kt_hi = jnp.where(lower_feat, 0.0, kt).astype(BF16)
    for j in range(tm // BLOCK):
        ktlo_ref[j] = kt_lo[:, j * BLOCK:(j + 1) * BLOCK]
        kthi_ref[j] = kt_hi[:, j * BLOCK:(j + 1) * BLOCK]


def _in_proj(x2d, gain, w_cat, w_kt, cos_t, sin_t, row_tile):
    n, d = x2d.shape
    table_tiles = cos_t.shape[0] // row_tile
    kb_per_tile = row_tile // BLOCK
    row_spec = lambda w: pl.BlockSpec((row_tile, w), lambda i: (i, 0))
    const_spec = lambda shape: pl.BlockSpec(shape, lambda i: (0, 0), pipeline_mode=pl.Buffered(1))
    table_spec = pl.BlockSpec((row_tile, LANES), lambda i: (i % table_tiles, 0))
    kt_spec = pl.BlockSpec((kb_per_tile, WIDTH_B, BLOCK), lambda i: (i, 0, 0))
    row_out = lambda w: (row_spec(w), jax.ShapeDtypeStruct((n, w), BF16))
    kt_out = (kt_spec, jax.ShapeDtypeStruct((n // BLOCK, WIDTH_B, BLOCK), BF16))
    outs = [row_out(WIDTH_A), row_out(KV_DUP_A), row_out(KV_DUP_A), row_out(KV_DUP_A),
            row_out(WIDTH_B), kt_out, kt_out, row_out(WIDTH_B), row_out(WIDTH_B)]
    return pl.pallas_call(
        _in_proj_kernel,
        grid=(n // row_tile,),
        in_specs=[row_spec(d), const_spec((1, d)), const_spec(w_cat.shape), const_spec(w_kt.shape),
                  table_spec, table_spec],
        out_specs=[o[0] for o in outs],
        out_shape=[o[1] for o in outs],
        compiler_params=pltpu.CompilerParams(
            dimension_semantics=("arbitrary",), vmem_limit_bytes=VMEM_LIMIT),
        name="in_proj",
    )(x2d, gain, w_cat, w_kt, cos_t, sin_t)


def _stack_pair(q2):
    qf = q2.astype(F32)
    lower = lax.broadcasted_iota(jnp.int32, qf.shape, 1) < HEAD_DIM
    return jnp.concatenate([jnp.where(lower, qf, 0.0), jnp.where(lower, 0.0, qf)], axis=0).astype(BF16)


def _unstack(p):
    return jnp.concatenate([p[:BLOCK], p[BLOCK:]], axis=1)


def _attention_kernel(sinks_ref, qa_ref, ka_ref, valo_ref, vahi_ref, qb_ref, ktlo_ref, kthi_ref,
                      vblo_ref, vbhi_ref, mka_ref, mvalo_ref, mvahi_ref, mktlo_ref, mkthi_ref,
                      mvblo_ref, mvbhi_ref, ga_ref, gb_ref, tri_ref, out_ref, carry_ref, oacc_ref):
    qi = pl.program_id(1)
    blocks_per_tile = Q_TILE // BLOCK

    row = lax.broadcasted_iota(jnp.int32, (2 * BLOCK, LANES), 0)
    qq = row & (BLOCK - 1)
    kk = lax.broadcasted_iota(jnp.int32, (2 * BLOCK, LANES), 1)
    upper_rows = lax.broadcasted_iota(jnp.int32, (2 * BLOCK, 1), 0) >= BLOCK
    lower_lanes = lax.broadcasted_iota(jnp.int32, (BLOCK, LANES), 1) < HEAD_DIM
    is_meta = kk >= PAD
    in_cur = kk <= qq
    neg_inf = -jnp.inf
    for half in range(blocks_per_tile):
        blk = qi * blocks_per_tile + half
        cur0 = pl.multiple_of(blk * BLOCK, BLOCK)
        prev0 = pl.multiple_of(jnp.maximum(blk - 1, 0) * BLOCK, BLOCK)
        has_prev = blk >= 1
        rows = slice(half * BLOCK, (half + 1) * BLOCK)
        oa = []
        for g in range(N_PAIRS_A):
            c = (2 * g) // (N_HEADS_A // N_KV_A)
            csl = slice(c * LANES, (c + 1) * LANES)
            qs = _stack_pair(qa_ref[0, rows, g * LANES:(g + 1) * LANES])
            s_cur = _dot_nt(qs, ka_ref[0, pl.ds(cur0, BLOCK), csl])
            s_prev = _dot_nt(qs, ka_ref[0, pl.ds(prev0, BLOCK), csl])
            s_meta = _dot_nt(qs, mka_ref[:, csl])
            s = jnp.where(in_cur, s_cur, jnp.where(has_prev, s_prev, neg_inf))
            sm = jnp.where(is_meta, s_meta, neg_inf)
            sink = jnp.where(upper_rows, sinks_ref[2 * g + 1], sinks_ref[2 * g])
            m = jnp.maximum(jnp.maximum(jnp.max(s, axis=-1, keepdims=True),
                                        jnp.max(sm, axis=-1, keepdims=True)), sink)
            p = jnp.exp(s - m)
            pm = jnp.exp(sm - m)
            den = (jnp.sum(p, axis=-1, keepdims=True) + jnp.sum(pm, axis=-1, keepdims=True)
                   + jnp.exp(sink - m))
            inv = 1.0 / den
            p_cur = jnp.where(in_cur, p, 0.0).astype(BF16)
            p_prev = jnp.where(in_cur, 0.0, p).astype(BF16)
            pcat = jnp.concatenate([_unstack(p_prev), _unstack(p_cur), _unstack(pm.astype(BF16))], axis=1)
            vcat = jnp.concatenate(
                [valo_ref[0, pl.ds(prev0, BLOCK), csl], vahi_ref[0, pl.ds(prev0, BLOCK), csl],
                 valo_ref[0, pl.ds(cur0, BLOCK), csl], vahi_ref[0, pl.ds(cur0, BLOCK), csl],
                 mvalo_ref[:, csl], mvahi_ref[:, csl]], axis=0)
            o2 = _dot(pcat, vcat)
            oa.append(o2 * jnp.where(lower_lanes, inv[:BLOCK], inv[BLOCK:]))
        oa = jnp.concatenate(oa, axis=1)
        out_ref[0, rows, :WIDTH_A] = _rms(oa, ga_ref[...]).astype(out_ref.dtype)

    carry_ref[...] = jnp.zeros_like(carry_ref)
    oacc_ref[...] = jnp.zeros_like(oacc_ref)
    tri = tri_ref[...]
    k_minus_q = (lax.broadcasted_iota(jnp.int32, (Q_TILE, LANES), 1)
                 - lax.broadcasted_iota(jnp.int32, (Q_TILE, LANES), 0))
    meta_valid = lax.broadcasted_iota(jnp.int32, (Q_TILE, LANES), 1) >= PAD

    def b_step(load_kt, load_v, valid):
        zs = []
        for g in range(N_PAIRS_B):
            gsl = slice(g * LANES, (g + 1) * LANES)
            zs.append(_dot(qb_ref[0, :, gsl], jnp.concatenate(load_kt(gsl), axis=1)))
        rs = []
        for g in range(N_PAIRS_B):
            z = zs[g]
            neg_abs = lax.bitcast_convert_type(
                lax.bitcast_convert_type(z, jnp.uint32) | jnp.uint32(0x80000000), F32)
            sp = jnp.maximum(z, 0.0) + jnp.log(1.0 + jnp.exp2(neg_abs)) * LOG2_E
            r_heads = []
            for h in range(2):
                sp_h = sp[:, h * BLOCK:(h + 1) * BLOCK]
                if valid is not None:
                    sp_h = jnp.where(valid, sp_h, 0.0)
                hi = sp_h.astype(BF16)
                lo = (sp_h - hi.astype(F32)).astype(BF16)
                r_heads.append(_dot(jnp.concatenate([hi, lo], axis=1), tri))
            rs.append(r_heads)
        for g in range(N_PAIRS_B):
            gsl = slice(g * LANES, (g + 1) * LANES)
            a_heads = []
            for h in range(2):
                r = rs[g][h]
                carry = carry_ref[g, h]
                a = jnp.exp2(zs[g][:, h * BLOCK:(h + 1) * BLOCK] + r[:, :BLOCK] + carry)
                if valid is not None:
                    a = jnp.where(valid, a, 0.0)
                carry_ref[g, h] = carry + r[:, BLOCK:]
                a_heads.append(a.astype(BF16))
            oacc_ref[:, gsl] += _dot(jnp.concatenate(a_heads, axis=1),
                                     jnp.concatenate(load_v(gsl), axis=0))

    def real_block(kb, valid):
        r0 = pl.multiple_of(kb * BLOCK, BLOCK)
        b_step(lambda gsl: (ktlo_ref[0, kb, gsl, :], kthi_ref[0, kb, gsl, :]),
               lambda gsl: (vblo_ref[0, pl.ds(r0, BLOCK), gsl], vbhi_ref[0, pl.ds(r0, BLOCK), gsl]),
               valid)

    top = qi * blocks_per_tile + blocks_per_tile - 1

    def diag_body(j, c):
        real_block(top - j, k_minus_q < (j + 1 - blocks_per_tile) * BLOCK)
        return c

    lax.fori_loop(0, blocks_per_tile, diag_body, 0)

    def full_body(j, c):
        real_block(qi * blocks_per_tile - 1 - j, None)
        return c

    lax.fori_loop(0, qi * blocks_per_tile, full_body, 0)
    b_step(lambda gsl: (mktlo_ref[0, gsl, :], mkthi_ref[0, gsl, :]),
           lambda gsl: (mvblo_ref[:, gsl], mvbhi_ref[:, gsl]), meta_valid)
    out_ref[0, :, WIDTH_A:] = _rms(oacc_ref[...], gb_ref[...]).astype(out_ref.dtype)


def _attention(sinks, real, meta, gain_a, gain_b, tri, batch, seq):
    nq = seq // Q_TILE
    nkb = seq // BLOCK
    qa, ka, valo, vahi, qb, ktlo, kthi, vblo, vbhi = real
    rows3 = lambda t: t.reshape(batch, seq, t.shape[-1])
    qa, ka, valo, vahi, qb, vblo, vbhi = [rows3(t) for t in (qa, ka, valo, vahi, qb, vblo, vbhi)]
    ktlo, kthi = [t.reshape(batch, nkb, WIDTH_B, BLOCK) for t in (ktlo, kthi)]
    _, mka, mvalo, mvahi, _, mktlo, mkthi, mvblo, mvbhi = meta
    qspec = lambda w: pl.BlockSpec((1, Q_TILE, w), lambda b, i: (b, i, 0))
    seqspec = lambda w: pl.BlockSpec((1, seq, w), lambda b, i: (b, 0, 0))
    ktspec = pl.BlockSpec((1, nkb, WIDTH_B, BLOCK), lambda b, i: (b, 0, 0, 0))
    const = lambda a: pl.BlockSpec(a.shape, lambda b, i: (0,) * a.ndim)
    mix = WIDTH_A + WIDTH_B
    return pl.pallas_call(
        _attention_kernel,
        grid=(batch, nq),
        in_specs=[pl.BlockSpec(memory_space=pltpu.SMEM),
                  qspec(WIDTH_A), seqspec(KV_DUP_A), seqspec(KV_DUP_A), seqspec(KV_DUP_A),
                  qspec(WIDTH_B), ktspec, ktspec, seqspec(WIDTH_B), seqspec(WIDTH_B),
                  const(mka), const(mvalo), const(mvahi), const(mktlo), const(mkthi),
                  const(mvblo), const(mvbhi), const(gain_a), const(gain_b), const(tri)],
        out_specs=pl.BlockSpec((1, Q_TILE, mix), lambda b, i: (b, i, 0)),
        out_shape=jax.ShapeDtypeStruct((batch, seq, mix), BF16),
        scratch_shapes=[pltpu.VMEM((N_PAIRS_B, 2, Q_TILE, LANES), F32),
                        pltpu.VMEM((Q_TILE, WIDTH_B), F32)],
        compiler_params=pltpu.CompilerParams(
            dimension_semantics=("arbitrary", "arbitrary"), vmem_limit_bytes=VMEM_LIMIT),
        name="attention",
    )(sinks, qa, ka, valo, vahi, qb, ktlo, kthi, vblo, vbhi, mka, mvalo, mvahi, mktlo, mkthi,
      mvblo, mvbhi, gain_a, gain_b, tri)


def _out_ffn_kernel(ff_chunks, mixed_ref, x_ref, wo_ref, gf_ref, wg_ref, wu_ref, wd_ref, gl_ref, out_ref):
    h = x_ref[...] + _dot(mixed_ref[...], wo_ref[...])
    u = _rms(h, gf_ref[...]).astype(BF16)
    y = h
    for c0, c1 in ff_chunks:
        gate = _dot(u, wg_ref[:, c0:c1])
        up = _dot(u, wu_ref[:, c0:c1])
        act = (gate * jax.nn.sigmoid(gate) * up).astype(BF16)
        y = y + _dot(act, wd_ref[c0:c1, :])
    out_ref[...] = _rms(y, gl_ref[...])


def _ff_chunks(d_ff, n_chunks, col_tile):
    tiles = -(-d_ff // col_tile)
    bounds = [min(d_ff, (tiles * k // n_chunks) * col_tile) for k in range(n_chunks + 1)]
    return tuple((bounds[k], bounds[k + 1]) for k in range(n_chunks))


def _out_ffn(mixed2d, x2d, w_out, g_ffn, w_gate, w_up, w_down, g_final, row_tile):
    n, d = x2d.shape
    d_ff = w_gate.shape[1]
    row_spec = lambda w: pl.BlockSpec((row_tile, w), lambda i: (i, 0))
    const = lambda a: pl.BlockSpec(a.shape, lambda i: (0, 0), pipeline_mode=pl.Buffered(1))
    mxu_cols = 256
    return pl.pallas_call(
        functools.partial(_out_ffn_kernel, _ff_chunks(d_ff, 2, mxu_cols)),
        grid=(n // row_tile,),
        in_specs=[row_spec(mixed2d.shape[1]), row_spec(d), const(w_out), const(g_ffn),
                  const(w_gate), const(w_up), const(w_down), const(g_final)],
        out_specs=row_spec(d),
        out_shape=jax.ShapeDtypeStruct((n, d), x2d.dtype),
        compiler_params=pltpu.CompilerParams(
            dimension_semantics=("arbitrary",), vmem_limit_bytes=VMEM_LIMIT),
        name="out_ffn",
    )(mixed2d, x2d, w_out, g_ffn, w_gate, w_up, w_down, g_final)


def _rope_tables(pos):
    half = HEAD_DIM // 2
    inv_freq = ROPE_THETA ** (-jnp.arange(half, dtype=F32) / half)
    ang = pos.astype(F32)[:, None] * inv_freq[None, :]
    cos, sin = jnp.cos(ang), jnp.sin(ang)
    reps = LANES // HEAD_DIM
    return (jnp.tile(jnp.concatenate([cos, cos], axis=1), (1, reps)),
            jnp.tile(jnp.concatenate([-sin, sin], axis=1), (1, reps)))


def _rearranged_w_in(w):
    o_ka = WIDTH_A
    o_va = o_ka + KV_WIDTH_A
    o_qb = o_va + KV_WIDTH_A
    o_kb = o_qb + WIDTH_B
    o_vb = o_kb + WIDTH_B
    dup = lambda t: jnp.repeat(t.reshape(t.shape[0], N_KV_A, 1, HEAD_DIM), 2, axis=2).reshape(t.shape[0], KV_DUP_A)
    w_cat = jnp.concatenate([w[:, :o_ka], dup(w[:, o_ka:o_va]), dup(w[:, o_va:o_qb]),
                             w[:, o_qb:o_kb], w[:, o_vb:]], axis=1).astype(BF16)
    return w_cat, w[:, o_kb:o_vb].T.astype(BF16)


def kernel(x, meta_tokens, norm_mix, w_in, sinks, norm_out_a, norm_out_b, w_out,
           norm_ffn, w_gate, w_up, w_down, norm_final):
    batch, seq, d = x.shape
    assert norm_mix.shape[0] == 1, "single-layer block: meta rows are not carried past the mixer"
    assert seq % ROW_TILE_PROJ == 0 and (batch * seq) % ROW_TILE_FFN == 0 and seq % Q_TILE == 0
    x2d = x.reshape(batch * seq, d)
    w_cat, w_kt = _rearranged_w_in(w_in[0])
    assert w_cat.shape[1] == _C_END
    g_mix = norm_mix[0].reshape(1, d)

    cos_r, sin_r = _rope_tables(jnp.arange(seq) + N_META)
    cos_m, sin_m = _rope_tables(jnp.arange(BLOCK) - PAD)
    meta_block = jnp.concatenate([jnp.zeros((PAD, d), x.dtype), meta_tokens.astype(x.dtype)], axis=0)

    real = _in_proj(x2d, g_mix, w_cat, w_kt, cos_r, sin_r, ROW_TILE_PROJ)
    meta = _in_proj(meta_block, g_mix, w_cat, w_kt, cos_m, sin_m, BLOCK)

    j = jnp.arange(2 * BLOCK)[:, None] % BLOCK
    s = jnp.arange(2 * BLOCK)[None, :]
    tri = jnp.where((s >= BLOCK) | (j >= s), -1.0, 0.0).astype(BF16)

    mixed = _attention(sinks[0], real, meta, norm_out_a[0].reshape(1, WIDTH_A),
                       norm_out_b[0].reshape(1, WIDTH_B), tri, batch, seq)

    out = _out_ffn(mixed.reshape(batch * seq, WIDTH_A + WIDTH_B), x2d, w_out[0].astype(BF16),
                   norm_ffn[0].reshape(1, d), w_gate[0].astype(BF16), w_up[0].astype(BF16),
                   w_down[0].astype(BF16), norm_final.reshape(1, d), ROW_TILE_FFN)
    return out.reshape(batch, seq, d)
```

```python
import functools

import jax
import jax.numpy as jnp
from jax import lax
from jax.experimental import pallas as pl
from jax.experimental.pallas import tpu as pltpu

HEAD_DIM = 64
N_HEADS_A = 8
N_KV_A = 2
N_HEADS_B = 8
WIDTH_A = N_HEADS_A * HEAD_DIM
KV_WIDTH_A = N_KV_A * HEAD_DIM
WIDTH_B = N_HEADS_B * HEAD_DIM
BLOCK = 128
N_META = 16
PAD = BLOCK - N_META
ROPE_THETA = 10000.0
EPS = 1e-6
LANES = 128
N_PAIRS_A = WIDTH_A // LANES
N_PAIRS_B = WIDTH_B // LANES
KV_DUP_A = 2 * KV_WIDTH_A

Q_TILE = 2 * BLOCK
ROW_TILE_PROJ = 512
ROW_TILE_FFN = 512
VMEM_LIMIT = 56 * 1024 * 1024

F32 = jnp.float32
BF16 = jnp.bfloat16
LOG2_E = 1.4426950408889634
DEAD_LOG2 = -150.0

_C_QA = 0
_C_KA = _C_QA + WIDTH_A
_C_VA = _C_KA + KV_DUP_A
_C_QB = _C_VA + KV_DUP_A
_C_VB = _C_QB + WIDTH_B
_C_END = _C_VB + WIDTH_B


def _rms(xf, g):
    ms = jnp.mean(xf * xf, axis=-1, keepdims=True)
    return xf * lax.rsqrt(ms + EPS) * g


def _dot(a, b):
    return jnp.dot(a, b, preferred_element_type=F32)


def _dot_nt(a, b):
    return lax.dot_general(a, b, (((1,), (1,)), ((), ())), preferred_element_type=F32)


def _in_proj_kernel(x_ref, g_ref, w_ref, wkt_ref, cos_ref, sin_ref,
                    qa_ref, ka_ref, valo_ref, vahi_ref, qb_ref, ktlo_ref, kthi_ref, vblo_ref, vbhi_ref):
    tm = x_ref.shape[0]
    u = _rms(x_ref[...], g_ref[...]).astype(BF16)
    proj = _dot(u, w_ref[...])
    cos = cos_ref[...]
    sin = sin_ref[...]
    lane = lax.broadcasted_iota(jnp.int32, (tm, LANES), 1)
    first_half = (lane & (HEAD_DIM // 2)) == 0
    lower_head = lane < HEAD_DIM
    scale = HEAD_DIM ** -0.5

    def rope(t):
        rot = jnp.where(first_half,
                        pltpu.roll(t, LANES - HEAD_DIM // 2, 1),
                        pltpu.roll(t, HEAD_DIM // 2, 1))
        return t * cos + rot * sin

    def group(c0, j):
        return proj[:, c0 + j * LANES:c0 + (j + 1) * LANES]

    for j in range(N_PAIRS_A):
        sl = slice(j * LANES, (j + 1) * LANES)
        qa_ref[:, sl] = (rope(group(_C_QA, j)) * scale).astype(BF16)
    for j in range(KV_DUP_A // LANES):
        sl = slice(j * LANES, (j + 1) * LANES)
        ka_ref[:, sl] = rope(group(_C_KA, j)).astype(BF16)
        v = group(_C_VA, j)
        valo_ref[:, sl] = jnp.where(lower_head, v, 0.0).astype(BF16)
        vahi_ref[:, sl] = jnp.where(lower_head, 0.0, v).astype(BF16)
    for j in range(N_PAIRS_B):
        sl = slice(j * LANES, (j + 1) * LANES)
        qb_ref[:, sl] = (group(_C_QB, j) * (scale * LOG2_E)).astype(BF16)
        v = group(_C_VB, j)
        vblo_ref[:, sl] = jnp.where(lower_head, v, 0.0).astype(BF16)
        vbhi_ref[:, sl] = jnp.where(lower_head, 0.0, v).astype(BF16)

    kt = _dot_nt(wkt_ref[...], u)
    lower_feat = (lax.broadcasted_iota(jnp.int32, kt.shape, 0) & HEAD_DIM) == 0
    kt_lo = jnp.where(lower_feat, kt, 0.0).astype(BF16)
    kt_hi = jnp.where(lower_feat, 0.0, kt).astype(BF16)
    for j in range(tm // BLOCK):
        ktlo_ref[j] = kt_lo[:, j * BLOCK:(j + 1) * BLOCK]
        kthi_ref[j] = kt_hi[:, j * BLOCK:(j + 1) * BLOCK]


def _in_proj(x2d, gain, w_cat, w_kt, cos_t, sin_t, row_tile):
    n, d = x2d.shape
    table_tiles = cos_t.shape[0] // row_tile
    kb_per_tile = row_tile // BLOCK
    row_spec = lambda w: pl.BlockSpec((row_tile, w), lambda i: (i, 0))
    const_spec = lambda shape: pl.BlockSpec(shape, lambda i: (0, 0), pipeline_mode=pl.Buffered(1))
    table_spec = pl.BlockSpec((row_tile, LANES), lambda i: (i % table_tiles, 0))
    kt_spec = pl.BlockSpec((kb_per_tile, WIDTH_B, BLOCK), lambda i: (i, 0, 0))
    row_out = lambda w: (row_spec(w), jax.ShapeDtypeStruct((n, w), BF16))
    kt_out = (kt_spec, jax.ShapeDtypeStruct((n // BLOCK, WIDTH_B, BLOCK), BF16))
    outs = [row_out(WIDTH_A), row_out(KV_DUP_A), row_out(KV_DUP_A), row_out(KV_DUP_A),
            row_out(WIDTH_B), kt_out, kt_out, row_out(WIDTH_B), row_out(WIDTH_B)]
    return pl.pallas_call(
        _in_proj_kernel,
        grid=(n // row_tile,),
        in_specs=[row_spec(d), const_spec((1, d)), const_spec(w_cat.shape), const_spec(w_kt.shape),
                  table_spec, table_spec],
        out_specs=[o[0] for o in outs],
        out_shape=[o[1] for o in outs],
        compiler_params=pltpu.CompilerParams(
            dimension_semantics=("arbitrary",), vmem_limit_bytes=VMEM_LIMIT),
        name="in_proj",
    )(x2d, gain, w_cat, w_kt, cos_t, sin_t)


def _stack_pair(q2):
    qf = q2.astype(F32)
    lower = lax.broadcasted_iota(jnp.int32, qf.shape, 1) < HEAD_DIM
    return jnp.concatenate([jnp.where(lower, qf, 0.0), jnp.where(lower, 0.0, qf)], axis=0).astype(BF16)


def _unstack(p):
    return jnp.concatenate([p[:BLOCK], p[BLOCK:]], axis=1)


def _attention_kernel(sinks_ref, qa_ref, ka_ref, valo_ref, vahi_ref, qb_ref, ktlo_ref, kthi_ref,
                      vblo_ref, vbhi_ref, mka_ref, mvalo_ref, mvahi_ref, mktlo_ref, mkthi_ref,
                      mvblo_ref, mvbhi_ref, ga_ref, gb_ref, tri_ref, out_ref, carry_ref, oacc_ref):
    qi = pl.program_id(1)
    blocks_per_tile = Q_TILE // BLOCK

    row = lax.broadcasted_iota(jnp.int32, (2 * BLOCK, LANES), 0)
    qq = row & (BLOCK - 1)
    kk = lax.broadcasted_iota(jnp.int32, (2 * BLOCK, LANES), 1)
    upper_rows = lax.broadcasted_iota(jnp.int32, (2 * BLOCK, 1), 0) >= BLOCK
    lower_lanes = lax.broadcasted_iota(jnp.int32, (BLOCK, LANES), 1) < HEAD_DIM
    is_meta = kk >= PAD
    in_cur = kk <= qq
    neg_inf = -jnp.inf
    for half in range(blocks_per_tile):
        blk = qi * blocks_per_tile + half
        cur0 = pl.multiple_of(blk * BLOCK, BLOCK)
        prev0 = pl.multiple_of(jnp.maximum(blk - 1, 0) * BLOCK, BLOCK)
        has_prev = blk >= 1
        rows = slice(half * BLOCK, (half + 1) * BLOCK)
        oa = []
        for g in range(N_PAIRS_A):
            c = (2 * g) // (N_HEADS_A // N_KV_A)
            csl = slice(c * LANES, (c + 1) * LANES)
            qs = _stack_pair(qa_ref[0, rows, g * LANES:(g + 1) * LANES])
            s_cur = _dot_nt(qs, ka_ref[0, pl.ds(cur0, BLOCK), csl])
            s_prev = _dot_nt(qs, ka_ref[0, pl.ds(prev0, BLOCK), csl])
            s_meta = _dot_nt(qs, mka_ref[:, csl])
            s = jnp.where(in_cur, s_cur, jnp.where(has_prev, s_prev, neg_inf))
            sm = jnp.where(is_meta, s_meta, neg_inf)
            sink = jnp.where(upper_rows, sinks_ref[2 * g + 1], sinks_ref[2 * g])
            m = jnp.maximum(jnp.maximum(jnp.max(s, axis=-1, keepdims=True),
                                        jnp.max(sm, axis=-1, keepdims=True)), sink)
            p = jnp.exp(s - m)
            pm = jnp.exp(sm - m)
            den = (jnp.sum(p, axis=-1, keepdims=True) + jnp.sum(pm, axis=-1, keepdims=True)
                   + jnp.exp(sink - m))
            inv = 1.0 / den
            p_cur = jnp.where(in_cur, p, 0.0).astype(BF16)
            p_prev = jnp.where(in_cur, 0.0, p).astype(BF16)
            pcat = jnp.concatenate([_unstack(p_prev), _unstack(p_cur), _unstack(pm.astype(BF16))], axis=1)
            vcat = jnp.concatenate(
                [valo_ref[0, pl.ds(prev0, BLOCK), csl], vahi_ref[0, pl.ds(prev0, BLOCK), csl],
                 valo_ref[0, pl.ds(cur0, BLOCK), csl], vahi_ref[0, pl.ds(cur0, BLOCK), csl],
                 mvalo_ref[:, csl], mvahi_ref[:, csl]], axis=0)
            o2 = _dot(pcat, vcat)
            oa.append(o2 * jnp.where(lower_lanes, inv[:BLOCK], inv[BLOCK:]))
        oa = jnp.concatenate(oa, axis=1)
        out_ref[0, rows, :WIDTH_A] = _rms(oa, ga_ref[...]).astype(out_ref.dtype)

    carry_ref[...] = jnp.zeros_like(carry_ref)
    oacc_ref[...] = jnp.zeros_like(oacc_ref)
    tri = tri_ref[...]
    k_minus_q = (lax.broadcasted_iota(jnp.int32, (Q_TILE, LANES), 1)
                 - lax.broadcasted_iota(jnp.int32, (Q_TILE, LANES), 0))
    meta_valid = lax.broadcasted_iota(jnp.int32, (Q_TILE, LANES), 1) >= PAD

    def b_step(load_kt, load_v, valid, want_carry_max=False):
        zs = []
        for g in range(N_PAIRS_B):
            gsl = slice(g * LANES, (g + 1) * LANES)
            zs.append(_dot(qb_ref[0, :, gsl], jnp.concatenate(load_kt(gsl), axis=1)))
        rs = []
        for g in range(N_PAIRS_B):
            z = zs[g]
            sp = jnp.maximum(z, 0.0) + jnp.log2(1.0 + jnp.exp2(-jnp.abs(z)))
            r_heads = []
            for h in range(2):
                sp_h = sp[:, h * BLOCK:(h + 1) * BLOCK]
                if valid is not None:
                    sp_h = jnp.where(valid, sp_h, 0.0)
                hi = sp_h.astype(BF16)
                lo = (sp_h - hi.astype(F32)).astype(BF16)
                r_heads.append(_dot(jnp.concatenate([hi, lo], axis=1), tri))
            rs.append(r_heads)
        carry_max = None
        for g in range(N_PAIRS_B):
            gsl = slice(g * LANES, (g + 1) * LANES)
            a_heads = []
            for h in range(2):
                r = rs[g][h]
                carry = carry_ref[g, h]
                a = jnp.exp2(zs[g][:, h * BLOCK:(h + 1) * BLOCK] + r[:, :BLOCK] + carry)
                if valid is not None:
                    a = jnp.where(valid, a, 0.0)
                carry = carry + r[:, BLOCK:]
                carry_ref[g, h] = carry
                if want_carry_max:
                    carry_max = carry if carry_max is None else jnp.maximum(carry_max, carry)
                a_heads.append(a.astype(BF16))
            oacc_ref[:, gsl] += _dot(jnp.concatenate(a_heads, axis=1),
                                     jnp.concatenate(load_v(gsl), axis=0))
        return jnp.max(carry_max) if want_carry_max else None

    def real_block(kb, valid, want_carry_max=False):
        r0 = pl.multiple_of(kb * BLOCK, BLOCK)
        return b_step(lambda gsl: (ktlo_ref[0, kb, gsl, :], kthi_ref[0, kb, gsl, :]),
                      lambda gsl: (vblo_ref[0, pl.ds(r0, BLOCK), gsl], vbhi_ref[0, pl.ds(r0, BLOCK), gsl]),
                      valid, want_carry_max)

    top = qi * blocks_per_tile + blocks_per_tile - 1

    def diag_body(j, c):
        real_block(top - j, k_minus_q < (j + 1 - blocks_per_tile) * BLOCK)
        return c

    lax.fori_loop(0, blocks_per_tile, diag_body, 0)

    n_full = qi * blocks_per_tile

    def full_cond(state):
        j, live = state
        return jnp.logical_and(j < n_full, live > 0)

    def full_body(state):
        j, _ = state
        carry_max = real_block(n_full - 1 - j, None, want_carry_max=True)
        return j + 1, (carry_max >= DEAD_LOG2).astype(jnp.int32)

    _, live = lax.while_loop(full_cond, full_body, (jnp.int32(0), jnp.int32(1)))

    @pl.when(live > 0)
    def _():
        b_step(lambda gsl: (mktlo_ref[0, gsl, :], mkthi_ref[0, gsl, :]),
               lambda gsl: (mvblo_ref[:, gsl], mvbhi_ref[:, gsl]), meta_valid)

    out_ref[0, :, WIDTH_A:] = _rms(oacc_ref[...], gb_ref[...]).astype(out_ref.dtype)


def _attention(sinks, real, meta, gain_a, gain_b, tri, batch, seq):
    nq = seq // Q_TILE
    nkb = seq // BLOCK
    qa, ka, valo, vahi, qb, ktlo, kthi, vblo, vbhi = real
    rows3 = lambda t: t.reshape(batch, seq, t.shape[-1])
    qa, ka, valo, vahi, qb, vblo, vbhi = [rows3(t) for t in (qa, ka, valo, vahi, qb, vblo, vbhi)]
    ktlo, kthi = [t.reshape(batch, nkb, WIDTH_B, BLOCK) for t in (ktlo, kthi)]
    _, mka, mvalo, mvahi, _, mktlo, mkthi, mvblo, mvbhi = meta
    qspec = lambda w: pl.BlockSpec((1, Q_TILE, w), lambda b, i: (b, i, 0))
    seqspec = lambda w: pl.BlockSpec((1, seq, w), lambda b, i: (b, 0, 0))
    ktspec = pl.BlockSpec((1, nkb, WIDTH_B, BLOCK), lambda b, i: (b, 0, 0, 0))
    const = lambda a: pl.BlockSpec(a.shape, lambda b, i: (0,) * a.ndim)
    mix = WIDTH_A + WIDTH_B
    return pl.pallas_call(
        _attention_kernel,
        grid=(batch, nq),
        in_specs=[pl.BlockSpec(memory_space=pltpu.SMEM),
                  qspec(WIDTH_A), seqspec(KV_DUP_A), seqspec(KV_DUP_A), seqspec(KV_DUP_A),
                  qspec(WIDTH_B), ktspec, ktspec, seqspec(WIDTH_B), seqspec(WIDTH_B),
                  const(mka), const(mvalo), const(mvahi), const(mktlo), const(mkthi),
                  const(mvblo), const(mvbhi), const(gain_a), const(gain_b), const(tri)],
        out_specs=pl.BlockSpec((1, Q_TILE, mix), lambda b, i: (b, i, 0)),
        out_shape=jax.ShapeDtypeStruct((batch, seq, mix), BF16),
        scratch_shapes=[pltpu.VMEM((N_PAIRS_B, 2, Q_TILE, LANES), F32),
                        pltpu.VMEM((Q_TILE, WIDTH_B), F32)],
        compiler_params=pltpu.CompilerParams(
            dimension_semantics=("arbitrary", "arbitrary"), vmem_limit_bytes=VMEM_LIMIT),
        name="attention",
    )(sinks, qa, ka, valo, vahi, qb, ktlo, kthi, vblo, vbhi, mka, mvalo, mvahi, mktlo, mkthi,
      mvblo, mvbhi, gain_a, gain_b, tri)


def _out_ffn_kernel(ff_chunks, mixed_ref, x_ref, wo_ref, gf_ref, wg_ref, wu_ref, wd_ref, gl_ref, out_ref):
    h = x_ref[...] + _dot(mixed_ref[...], wo_ref[...])
    u = _rms(h, gf_ref[...]).astype(BF16)
    y = h
    for c0, c1 in ff_chunks:
        gate = _dot(u, wg_ref[:, c0:c1])
        up = _dot(u, wu_ref[:, c0:c1])
        act = (gate * jax.nn.sigmoid(gate) * up).astype(BF16)
        y = y + _dot(act, wd_ref[c0:c1, :])
    out_ref[...] = _rms(y, gl_ref[...])


def _ff_chunks(d_ff, n_chunks, col_tile):
    tiles = -(-d_ff // col_tile)
    bounds = [min(d_ff, (tiles * k // n_chunks) * col_tile) for k in range(n_chunks + 1)]
    return tuple((bounds[k], bounds[k + 1]) for k in range(n_chunks))


def _out_ffn(mixed2d, x2d, w_out, g_ffn, w_gate, w_up, w_down, g_final, row_tile):
    n, d = x2d.shape
    d_ff = w_gate.shape[1]
    row_spec = lambda w: pl.BlockSpec((row_tile, w), lambda i: (i, 0))
    const = lambda a: pl.BlockSpec(a.shape, lambda i: (0, 0), pipeline_mode=pl.Buffered(1))
    mxu_cols = 256
    return pl.pallas_call(
        functools.partial(_out_ffn_kernel, _ff_chunks(d_ff, 2, mxu_cols)),
        grid=(n // row_tile,),
        in_specs=[row_spec(mixed2d.shape[1]), row_spec(d), const(w_out), const(g_ffn),
                  const(w_gate), const(w_up), const(w_down), const(g_final)],
        out_specs=row_spec(d),
        out_shape=jax.ShapeDtypeStruct((n, d), x2d.dtype),
        compiler_params=pltpu.CompilerParams(
            dimension_semantics=("arbitrary",), vmem_limit_bytes=VMEM_LIMIT),
        name="out_ffn",
    )(mixed2d, x2d, w_out, g_ffn, w_gate, w_up, w_down, g_final)


def _rope_tables(pos):
    half = HEAD_DIM // 2
    inv_freq = ROPE_THETA ** (-jnp.arange(half, dtype=F32) / half)
    ang = pos.astype(F32)[:, None] * inv_freq[None, :]
    cos, sin = jnp.cos(ang), jnp.sin(ang)
    reps = LANES // HEAD_DIM
    return (jnp.tile(jnp.concatenate([cos, cos], axis=1), (1, reps)),
            jnp.tile(jnp.concatenate([-sin, sin], axis=1), (1, reps)))


def _rearranged_w_in(w):
    o_ka = WIDTH_A
    o_va = o_ka + KV_WIDTH_A
    o_qb = o_va + KV_WIDTH_A
    o_kb = o_qb + WIDTH_B
    o_vb = o_kb + WIDTH_B
    dup = lambda t: jnp.repeat(t.reshape(t.shape[0], N_KV_A, 1, HEAD_DIM), 2, axis=2).reshape(t.shape[0], KV_DUP_A)
    w_cat = jnp.concatenate([w[:, :o_ka], dup(w[:, o_ka:o_va]), dup(w[:, o_va:o_qb]),
                             w[:, o_qb:o_kb], w[:, o_vb:]], axis=1).astype(BF16)
    return w_cat, w[:, o_kb:o_vb].T.astype(BF16)


def kernel(x, meta_tokens, norm_mix, w_in, sinks, norm_out_a, norm_out_b, w_out,
           norm_ffn, w_gate, w_up, w_down, norm_final):
    batch, seq, d = x.shape
    assert norm_mix.shape[0] == 1, "single-layer block: meta rows are not carried past the mixer"
    assert seq % ROW_TILE_PROJ == 0 and (batch * seq) % ROW_TILE_FFN == 0 and seq % Q_TILE == 0
    x2d = x.reshape(batch * seq, d)
    w_cat, w_kt = _rearranged_w_in(w_in[0])
    assert w_cat.shape[1] == _C_END
    g_mix = norm_mix[0].reshape(1, d)

    cos_r, sin_r = _rope_tables(jnp.arange(seq) + N_META)
    cos_m, sin_m = _rope_tables(jnp.arange(BLOCK) - PAD)
    meta_block = jnp.concatenate([jnp.zeros((PAD, d), x.dtype), meta_tokens.astype(x.dtype)], axis=0)

    real = _in_proj(x2d, g_mix, w_cat, w_kt, cos_r, sin_r, ROW_TILE_PROJ)
    meta = _in_proj(meta_block, g_mix, w_cat, w_kt, cos_m, sin_m, BLOCK)

    j = jnp.arange(2 * BLOCK)[:, None] % BLOCK
    s = jnp.arange(2 * BLOCK)[None, :]
    tri = jnp.where((s >= BLOCK) | (j >= s), -1.0, 0.0).astype(BF16)

    mixed = _attention(sinks[0], real, meta, norm_out_a[0].reshape(1, WIDTH_A),
                       norm_out_b[0].reshape(1, WIDTH_B), tri, batch, seq)

    out = _out_ffn(mixed.reshape(batch * seq, WIDTH_A + WIDTH_B), x2d, w_out[0].astype(BF16),
                   norm_ffn[0].reshape(1, d), w_gate[0].astype(BF16), w_up[0].astype(BF16),
                   w_down[0].astype(BF16), norm_final.reshape(1, d), ROW_TILE_FFN)
    return out.reshape(batch, seq, d)
```

```python
import functools

import jax
import jax.numpy as jnp
from jax import lax
from jax.experimental import pallas as pl
from jax.experimental.pallas import tpu as pltpu

HEAD_DIM = 64
N_HEADS_A = 8
N_KV_A = 2
N_HEADS_B = 8
WIDTH_A = N_HEADS_A * HEAD_DIM
KV_WIDTH_A = N_KV_A * HEAD_DIM
WIDTH_B = N_HEADS_B * HEAD_DIM
BLOCK = 128
N_META = 16
PAD = BLOCK - N_META
ROPE_THETA = 10000.0
EPS = 1e-6
LANES = 128
N_PAIRS_A = WIDTH_A // LANES
N_PAIRS_B = WIDTH_B // LANES
KV_DUP_A = 2 * KV_WIDTH_A

Q_TILE = 2 * BLOCK
STRAIGHT_FULL = 2
ROW_TILE_PROJ = 512
ROW_TILE_FFN = 512
VMEM_LIMIT = 56 * 1024 * 1024

F32 = jnp.float32
BF16 = jnp.bfloat16
LOG2_E = 1.4426950408889634
DEAD_LOG2 = -150.0

_C_QA = 0
_C_VA = _C_QA + WIDTH_A
_C_QB = _C_VA + KV_DUP_A
_C_VB = _C_QB + WIDTH_B
_C_END = _C_VB + WIDTH_B


def _rms(xf, g):
    ms = jnp.mean(xf * xf, axis=-1, keepdims=True)
    return xf * lax.rsqrt(ms + EPS) * g


def _dot(a, b):
    return jnp.dot(a, b, preferred_element_type=F32)


def _dot_nt(a, b):
    return lax.dot_general(a, b, (((1,), (1,)), ((), ())), preferred_element_type=F32)


def _in_proj_kernel(x_ref, g_ref, w_ref, wkt_ref, cos_ref, sin_ref, cosf_ref, sinf_ref,
                    qa_ref, katlo_ref, kathi_ref, valo_ref, vahi_ref,
                    qb_ref, ktlo_ref, kthi_ref, vblo_ref, vbhi_ref):
    tm = x_ref.shape[0]
    u = _rms(x_ref[...], g_ref[...]).astype(BF16)
    proj = _dot(u, w_ref[...])
    cos = cos_ref[...]
    sin = sin_ref[...]
    lane = lax.broadcasted_iota(jnp.int32, (tm, LANES), 1)
    first_half = (lane & (HEAD_DIM // 2)) == 0
    lower_head = lane < HEAD_DIM
    scale = HEAD_DIM ** -0.5

    def rope(t):
        rot = jnp.where(first_half,
                        pltpu.roll(t, LANES - HEAD_DIM // 2, 1),
                        pltpu.roll(t, HEAD_DIM // 2, 1))
        return t * cos + rot * sin

    def group(c0, j):
        return proj[:, c0 + j * LANES:c0 + (j + 1) * LANES]

    for j in range(N_PAIRS_A):
        sl = slice(j * LANES, (j + 1) * LANES)
        qa_ref[:, sl] = (rope(group(_C_QA, j)) * scale).astype(BF16)
    for j in range(KV_DUP_A // LANES):
        sl = slice(j * LANES, (j + 1) * LANES)
        v = group(_C_VA, j)
        valo_ref[:, sl] = jnp.where(lower_head, v, 0.0).astype(BF16)
        vahi_ref[:, sl] = jnp.where(lower_head, 0.0, v).astype(BF16)
    for j in range(N_PAIRS_B):
        sl = slice(j * LANES, (j + 1) * LANES)
        qb_ref[:, sl] = (group(_C_QB, j) * (scale * LOG2_E)).astype(BF16)
        v = group(_C_VB, j)
        vblo_ref[:, sl] = jnp.where(lower_head, v, 0.0).astype(BF16)
        vbhi_ref[:, sl] = jnp.where(lower_head, 0.0, v).astype(BF16)

    kt_all = _dot_nt(wkt_ref[...], u)

    cos_f = cosf_ref[...]
    sin_f = sinf_ref[...]
    half = HEAD_DIM // 2
    zeros = jnp.zeros((HEAD_DIM, tm), F32)
    lo_parts, hi_parts = [], []
    for c in range(N_KV_A):
        x1 = kt_all[c * HEAD_DIM:c * HEAD_DIM + half]
        x2 = kt_all[c * HEAD_DIM + half:(c + 1) * HEAD_DIM]
        roped = jnp.concatenate([x1 * cos_f - x2 * sin_f, x2 * cos_f + x1 * sin_f], axis=0)
        lo_parts += [roped, zeros]
        hi_parts += [zeros, roped]
    kat_lo = jnp.concatenate(lo_parts, axis=0).astype(BF16)
    kat_hi = jnp.concatenate(hi_parts, axis=0).astype(BF16)

    kt = kt_all[KV_WIDTH_A:]
    lower_feat = (lax.broadcasted_iota(jnp.int32, kt.shape, 0) & HEAD_DIM) == 0
    kt_lo = jnp.where(lower_feat, kt, 0.0).astype(BF16)
    kt_hi = jnp.where(lower_feat, 0.0, kt).astype(BF16)
    for j in range(tm // BLOCK):
        cols = slice(j * BLOCK, (j + 1) * BLOCK)
        katlo_ref[j] = kat_lo[:, cols]
        kathi_ref[j] = kat_hi[:, cols]
        ktlo_ref[j] = kt_lo[:, cols]
        kthi_ref[j] = kt_hi[:, cols]


def _in_proj(x2d, gain, w_cat, w_kt, tables, row_tile):
    n, d = x2d.shape
    cos_t, sin_t, cos_f, sin_f = tables
    table_tiles = cos_t.shape[0] // row_tile
    kb_per_tile = row_tile // BLOCK
    row_spec = lambda w: pl.BlockSpec((row_tile, w), lambda i: (i, 0))
    const_spec = lambda shape: pl.BlockSpec(shape, lambda i: (0, 0), pipeline_mode=pl.Buffered(1))
    table_spec = pl.BlockSpec((row_tile, LANES), lambda i: (i % table_tiles, 0))
    ftable_spec = pl.BlockSpec((HEAD_DIM // 2, row_tile), lambda i: (0, i % table_tiles))
    row_out = lambda w: (row_spec(w), jax.ShapeDtypeStruct((n, w), BF16))
    kt_out = lambda w: (pl.BlockSpec((kb_per_tile, w, BLOCK), lambda i: (i, 0, 0)),
                        jax.ShapeDtypeStruct((n // BLOCK, w, BLOCK), BF16))
    outs = [row_out(WIDTH_A), kt_out(KV_DUP_A), kt_out(KV_DUP_A), row_out(KV_DUP_A), row_out(KV_DUP_A),
            row_out(WIDTH_B), kt_out(WIDTH_B), kt_out(WIDTH_B), row_out(WIDTH_B), row_out(WIDTH_B)]
    return pl.pallas_call(
        _in_proj_kernel,
        grid=(n // row_tile,),
        in_specs=[row_spec(d), const_spec((1, d)), const_spec(w_cat.shape), const_spec(w_kt.shape),
                  table_spec, table_spec, ftable_spec, ftable_spec],
        out_specs=[o[0] for o in outs],
        out_shape=[o[1] for o in outs],
        compiler_params=pltpu.CompilerParams(
            dimension_semantics=("arbitrary",), vmem_limit_bytes=VMEM_LIMIT),
        name="in_proj",
    )(x2d, gain, w_cat, w_kt, cos_t, sin_t, cos_f, sin_f)


def _attention_kernel(sinks_ref, qa_ref, katlo_ref, kathi_ref, valo_ref, vahi_ref,
                      qb_ref, ktlo_ref, kthi_ref, vblo_ref, vbhi_ref,
                      mwa_ref, mva_ref, onesa_ref, mktlo_ref, mkthi_ref, mvblo_ref, mvbhi_ref,
                      ga_ref, gb_ref, tri_ref, out_ref, carry_ref, oacc_ref, zbuf_ref):
    qi = pl.program_id(1)
    blocks_per_tile = Q_TILE // BLOCK
    assert blocks_per_tile == 2, "the group-B sweep peels exactly two diagonal key blocks"
    assert STRAIGHT_FULL == blocks_per_tile, "a tile has either no full key block or >= STRAIGHT_FULL"

    in_cur = ((lax.broadcasted_iota(jnp.int32, (BLOCK, 2 * BLOCK), 1) & (BLOCK - 1))
              <= lax.broadcasted_iota(jnp.int32, (BLOCK, 2 * BLOCK), 0))
    in_cur1 = in_cur[:, :BLOCK]
    lane = lax.broadcasted_iota(jnp.int32, (BLOCK, LANES), 1)
    meta_lo = lane < N_META
    meta_any = lane < 2 * N_META
    lower_lanes = lane < HEAD_DIM
    neg_inf = -jnp.inf
    chains = [(half, g) for half in range(blocks_per_tile) for g in range(N_PAIRS_A)]

    def a_blocks(half):
        blk = qi * blocks_per_tile + half
        return blk, jnp.maximum(blk - 1, 0)

    scores = []
    for half, g in chains:
        blk, prev = a_blocks(half)
        c = (2 * g) // (N_HEADS_A // N_KV_A)
        csl = slice(c * LANES, (c + 1) * LANES)
        q2 = qa_ref[0, half * BLOCK:(half + 1) * BLOCK, g * LANES:(g + 1) * LANES]
        s_cur = _dot(q2, jnp.concatenate([katlo_ref[0, blk, csl, :], kathi_ref[0, blk, csl, :]], axis=1))
        s_prev = _dot(q2, jnp.concatenate([katlo_ref[0, prev, csl, :], kathi_ref[0, prev, csl, :]], axis=1))
        scores.append((s_cur, s_prev, _dot(q2, mwa_ref[c])))
    probs = []
    for (half, g), (s_cur, s_prev, sm) in zip(chains, scores):
        blk, _ = a_blocks(half)
        s = jnp.where(in_cur, s_cur, jnp.where(blk >= 1, s_prev, neg_inf))
        s_lo, s_hi = s[:, :BLOCK], s[:, BLOCK:]
        sink_lo, sink_hi = sinks_ref[2 * g], sinks_ref[2 * g + 1]
        m_lo = jnp.maximum(jnp.max(jnp.maximum(s_lo, jnp.where(meta_lo, sm, neg_inf)),
                                   axis=-1, keepdims=True), sink_lo)
        m_hi = jnp.maximum(jnp.max(jnp.maximum(s_hi, jnp.where(meta_lo | ~meta_any, neg_inf, sm)),
                                   axis=-1, keepdims=True), sink_hi)
        p_lo = jnp.exp(s_lo - m_lo)
        p_hi = jnp.exp(s_hi - m_hi)
        pm = jnp.where(meta_any, jnp.exp(sm - jnp.where(meta_lo, m_lo, m_hi)), 0.0)
        pcat = jnp.concatenate([jnp.where(in_cur1, 0.0, p_lo), jnp.where(in_cur1, 0.0, p_hi),
                                jnp.where(in_cur1, p_lo, 0.0), jnp.where(in_cur1, p_hi, 0.0)],
                               axis=1).astype(BF16)
        sink_term = jnp.where(lower_lanes, jnp.exp(sink_lo - m_lo), jnp.exp(sink_hi - m_hi))
        probs.append((pcat, pm.astype(BF16), sink_term))
    ones_a = onesa_ref[...]
    for half in range(blocks_per_tile):
        blk, prev = a_blocks(half)
        cur0 = pl.multiple_of(blk * BLOCK, BLOCK)
        prev0 = pl.multiple_of(prev * BLOCK, BLOCK)
        oa = []
        for g in range(N_PAIRS_A):
            c = (2 * g) // (N_HEADS_A // N_KV_A)
            csl = slice(c * LANES, (c + 1) * LANES)
            pcat, pm, sink_term = probs[half * N_PAIRS_A + g]
            vcat = jnp.concatenate(
                [valo_ref[0, pl.ds(prev0, BLOCK), csl], vahi_ref[0, pl.ds(prev0, BLOCK), csl],
                 valo_ref[0, pl.ds(cur0, BLOCK), csl], vahi_ref[0, pl.ds(cur0, BLOCK), csl]], axis=0)
            res = _dot(pcat, jnp.concatenate([vcat, ones_a], axis=1)) + _dot(pm, mva_ref[c])
            oa.append(res[:, :LANES] / (res[:, LANES:] + sink_term))
        oa = jnp.concatenate(oa, axis=1)
        out_ref[0, half * BLOCK:(half + 1) * BLOCK, :WIDTH_A] = _rms(oa, ga_ref[...]).astype(out_ref.dtype)

    carry_ref[...] = jnp.zeros_like(carry_ref)
    oacc_ref[...] = jnp.zeros_like(oacc_ref)
    tri = tri_ref[...]
    k_minus_q = (lax.broadcasted_iota(jnp.int32, (Q_TILE, LANES), 1)
                 - lax.broadcasted_iota(jnp.int32, (Q_TILE, LANES), 0))
    meta_valid = lax.broadcasted_iota(jnp.int32, (Q_TILE, LANES), 1) >= PAD

    def pair_lanes(g):
        return slice(g * LANES, (g + 1) * LANES)

    def pair_scores(g, rows, load_kt):
        return _dot(qb_ref[0, rows, pair_lanes(g)], jnp.concatenate(load_kt(pair_lanes(g)), axis=1))

    def scores(rows, load_kt):
        return [pair_scores(g, rows, load_kt) for g in range(N_PAIRS_B)]

    def b_step(zs, load_v, valid, rows=slice(0, Q_TILE), want_carry_max=False, ahead=None):
        rs = []
        for g in range(N_PAIRS_B):
            z = zs[g]
            sp = jnp.maximum(z, 0.0) + jnp.log2(1.0 + jnp.exp2(-jnp.abs(z)))
            r_heads = []
            for h in range(2):
                sp_h = sp[:, h * BLOCK:(h + 1) * BLOCK]
                if valid is not None:
                    sp_h = jnp.where(valid, sp_h, 0.0)
                hi = sp_h.astype(BF16)
                lo = (sp_h - hi.astype(F32)).astype(BF16)
                r_heads.append(_dot(jnp.concatenate([hi, lo], axis=1), tri))
            rs.append(r_heads)
            if ahead is not None:
                slot, load_kt = ahead
                zbuf_ref[slot, g] = pair_scores(g, slice(0, Q_TILE), load_kt)
        carry_max = None
        for g in range(N_PAIRS_B):
            a_heads = []
            for h in range(2):
                r = rs[g][h]
                carry = carry_ref[g, h, rows]
                a = jnp.exp2(zs[g][:, h * BLOCK:(h + 1) * BLOCK] + r[:, :BLOCK] + carry)
                if valid is not None:
                    a = jnp.where(valid, a, 0.0)
                carry = carry + r[:, BLOCK:]
                carry_ref[g, h, rows] = carry
                if want_carry_max:
                    carry_max = carry if carry_max is None else jnp.maximum(carry_max, carry)
                a_heads.append(a.astype(BF16))
            oacc_ref[rows, pair_lanes(g)] += _dot(jnp.concatenate(a_heads, axis=1),
                                                  jnp.concatenate(load_v(pair_lanes(g)), axis=0))
        return jnp.max(carry_max) if want_carry_max else None

    def real_kt(kb):
        return lambda sl: (ktlo_ref[0, kb, sl, :], kthi_ref[0, kb, sl, :])

    def real_v(kb):
        r0 = pl.multiple_of(kb * BLOCK, BLOCK)
        return lambda sl: (vblo_ref[0, pl.ds(r0, BLOCK), sl], vbhi_ref[0, pl.ds(r0, BLOCK), sl])

    def stashed(slot):
        return [zbuf_ref[slot, g] for g in range(N_PAIRS_B)]

    all_rows = slice(0, Q_TILE)
    upper_rows = slice(BLOCK, Q_TILE)
    n_full = qi * blocks_per_tile
    top = n_full + 1
    b_step(scores(upper_rows, real_kt(top)), real_v(top), k_minus_q[upper_rows] < -BLOCK,
           rows=upper_rows, ahead=(0, real_kt(top - 1)))
    b_step(stashed(0), real_v(top - 1), k_minus_q < 0,
           ahead=(1, real_kt(jnp.maximum(n_full - 1, 0))))

    def full_step(j, valid):
        kb = jnp.maximum(n_full - 1 - j, 0)
        slot = (j + 1) & 1
        return b_step(stashed(slot), real_v(kb), valid, want_carry_max=True,
                      ahead=(1 - slot, real_kt(jnp.maximum(kb - 1, 0))))

    exists = k_minus_q < jnp.where(n_full >= STRAIGHT_FULL, Q_TILE + BLOCK, -(Q_TILE + BLOCK))
    for j in range(STRAIGHT_FULL):
        carry_max = full_step(j, exists)

    def full_cond(state):
        j, live = state
        return jnp.logical_and(j < n_full, live > 0)

    def full_body(state):
        j, _ = state
        return j + 1, (full_step(j, None) >= DEAD_LOG2).astype(jnp.int32)

    _, live = lax.while_loop(full_cond, full_body,
                             (jnp.int32(STRAIGHT_FULL), (carry_max >= DEAD_LOG2).astype(jnp.int32)))

    @pl.when(live > 0)
    def _():
        b_step(scores(all_rows, lambda sl: (mktlo_ref[0, sl, :], mkthi_ref[0, sl, :])),
               lambda sl: (mvblo_ref[:, sl], mvbhi_ref[:, sl]), meta_valid)

    out_ref[0, :, WIDTH_A:] = _rms(oacc_ref[...], gb_ref[...]).astype(out_ref.dtype)


def _attention(sinks, real, meta, gain_a, gain_b, tri, batch, seq):
    nq = seq // Q_TILE
    nkb = seq // BLOCK
    qa, katlo, kathi, valo, vahi, qb, ktlo, kthi, vblo, vbhi = real
    rows3 = lambda t: t.reshape(batch, seq, t.shape[-1])
    qa, valo, vahi, qb, vblo, vbhi = [rows3(t) for t in (qa, valo, vahi, qb, vblo, vbhi)]
    slabs = lambda t: t.reshape(batch, nkb, t.shape[-2], BLOCK)
    katlo, kathi, ktlo, kthi = [slabs(t) for t in (katlo, kathi, ktlo, kthi)]
    _, mkatlo, mkathi, mvalo, mvahi, _, mktlo, mkthi, mvblo, mvbhi = meta
    pairs = range(KV_DUP_A // LANES)
    pair = lambda c: slice(c * LANES, (c + 1) * LANES)
    mwa = jnp.stack([jnp.pad(jnp.concatenate([mkatlo[0, pair(c), PAD:], mkathi[0, pair(c), PAD:]], axis=1),
                             ((0, 0), (0, LANES - 2 * N_META))) for c in pairs])
    mva = jnp.stack([jnp.pad(jnp.concatenate([mvalo[PAD:, pair(c)], mvahi[PAD:, pair(c)]], axis=0),
                             ((0, LANES - 2 * N_META), (0, 0))) for c in pairs])
    lower_lane = jnp.arange(LANES)[None, :] < HEAD_DIM
    head_ones = lambda upper_row: (lower_lane != upper_row[:, None]).astype(BF16)
    ones_a = head_ones((jnp.arange(4 * BLOCK) // BLOCK) % 2 == 1)
    meta_row = jnp.arange(LANES)
    meta_ones = jnp.where((meta_row < 2 * N_META)[:, None], head_ones(meta_row >= N_META), 0).astype(BF16)
    mva = jnp.concatenate([mva, jnp.broadcast_to(meta_ones, mva.shape)], axis=2)
    qspec = lambda w: pl.BlockSpec((1, Q_TILE, w), lambda b, i: (b, i, 0))
    seqspec = lambda w: pl.BlockSpec((1, seq, w), lambda b, i: (b, 0, 0))
    ktspec = lambda w: pl.BlockSpec((1, nkb, w, BLOCK), lambda b, i: (b, 0, 0, 0))
    const = lambda a: pl.BlockSpec(a.shape, lambda b, i: (0,) * a.ndim)
    mix = WIDTH_A + WIDTH_B
    return pl.pallas_call(
        _attention_kernel,
        grid=(batch, nq),
        in_specs=[pl.BlockSpec(memory_space=pltpu.SMEM),
                  qspec(WIDTH_A), ktspec(KV_DUP_A), ktspec(KV_DUP_A), seqspec(KV_DUP_A), seqspec(KV_DUP_A),
                  qspec(WIDTH_B), ktspec(WIDTH_B), ktspec(WIDTH_B), seqspec(WIDTH_B), seqspec(WIDTH_B),
                  const(mwa), const(mva), const(ones_a), const(mktlo), const(mkthi),
                  const(mvblo), const(mvbhi), const(gain_a), const(gain_b), const(tri)],
        out_specs=pl.BlockSpec((1, Q_TILE, mix), lambda b, i: (b, i, 0)),
        out_shape=jax.ShapeDtypeStruct((batch, seq, mix), BF16),
        scratch_shapes=[pltpu.VMEM((N_PAIRS_B, 2, Q_TILE, LANES), F32),
                        pltpu.VMEM((Q_TILE, WIDTH_B), F32),
                        pltpu.VMEM((2, N_PAIRS_B, Q_TILE, 2 * BLOCK), F32)],
        compiler_params=pltpu.CompilerParams(
            dimension_semantics=("arbitrary", "arbitrary"), vmem_limit_bytes=VMEM_LIMIT),
        name="attention",
    )(sinks, qa, katlo, kathi, valo, vahi, qb, ktlo, kthi, vblo, vbhi, mwa, mva, ones_a, mktlo, mkthi,
      mvblo, mvbhi, gain_a, gain_b, tri)


def _out_ffn_kernel(ff_chunks, mixed_ref, x_ref, wo_ref, gf_ref, wg_ref, wu_ref, wd_ref, gl_ref, out_ref):
    h = x_ref[...] + _dot(mixed_ref[...], wo_ref[...])
    u = _rms(h, gf_ref[...]).astype(BF16)
    y = h
    for c0, c1 in ff_chunks:
        gate = _dot(u, wg_ref[:, c0:c1])
        up = _dot(u, wu_ref[:, c0:c1])
        act = (gate * jax.nn.sigmoid(gate) * up).astype(BF16)
        y = y + _dot(act, wd_ref[c0:c1, :])
    out_ref[...] = _rms(y, gl_ref[...])


def _ff_chunks(d_ff, n_chunks, col_tile):
    tiles = -(-d_ff // col_tile)
    bounds = [min(d_ff, (tiles * k // n_chunks) * col_tile) for k in range(n_chunks + 1)]
    return tuple((bounds[k], bounds[k + 1]) for k in range(n_chunks))


def _out_ffn(mixed2d, x2d, w_out, g_ffn, w_gate, w_up, w_down, g_final, row_tile):
    n, d = x2d.shape
    d_ff = w_gate.shape[1]
    row_spec = lambda w: pl.BlockSpec((row_tile, w), lambda i: (i, 0))
    const = lambda a: pl.BlockSpec(a.shape, lambda i: (0, 0), pipeline_mode=pl.Buffered(1))
    mxu_cols = 256
    return pl.pallas_call(
        functools.partial(_out_ffn_kernel, _ff_chunks(d_ff, 2, mxu_cols)),
        grid=(n // row_tile,),
        in_specs=[row_spec(mixed2d.shape[1]), row_spec(d), const(w_out), const(g_ffn),
                  const(w_gate), const(w_up), const(w_down), const(g_final)],
        out_specs=row_spec(d),
        out_shape=jax.ShapeDtypeStruct((n, d), x2d.dtype),
        compiler_params=pltpu.CompilerParams(
            dimension_semantics=("arbitrary",), vmem_limit_bytes=VMEM_LIMIT),
        name="out_ffn",
    )(mixed2d, x2d, w_out, g_ffn, w_gate, w_up, w_down, g_final)


def _rope_tables(pos):
    half = HEAD_DIM // 2
    inv_freq = ROPE_THETA ** (-jnp.arange(half, dtype=F32) / half)
    ang = pos.astype(F32)[:, None] * inv_freq[None, :]
    cos, sin = jnp.cos(ang), jnp.sin(ang)
    reps = LANES // HEAD_DIM
    return (jnp.tile(jnp.concatenate([cos, cos], axis=1), (1, reps)),
            jnp.tile(jnp.concatenate([-sin, sin], axis=1), (1, reps)), cos.T, sin.T)


def _rearranged_w_in(w):
    o_ka = WIDTH_A
    o_va = o_ka + KV_WIDTH_A
    o_qb = o_va + KV_WIDTH_A
    o_kb = o_qb + WIDTH_B
    o_vb = o_kb + WIDTH_B
    dup = lambda t: jnp.repeat(t.reshape(t.shape[0], N_KV_A, 1, HEAD_DIM), 2, axis=2).reshape(t.shape[0], KV_DUP_A)
    w_cat = jnp.concatenate([w[:, :o_ka], dup(w[:, o_va:o_qb]), w[:, o_qb:o_kb], w[:, o_vb:]],
                            axis=1).astype(BF16)
    w_kt = jnp.concatenate([w[:, o_ka:o_va], w[:, o_kb:o_vb]], axis=1).T.astype(BF16)
    return w_cat, w_kt


def kernel(x, meta_tokens, norm_mix, w_in, sinks, norm_out_a, norm_out_b, w_out,
           norm_ffn, w_gate, w_up, w_down, norm_final):
    batch, seq, d = x.shape
    assert norm_mix.shape[0] == 1, "single-layer block: meta rows are not carried past the mixer"
    assert seq % ROW_TILE_PROJ == 0 and (batch * seq) % ROW_TILE_FFN == 0 and seq % Q_TILE == 0
    x2d = x.reshape(batch * seq, d)
    w_cat, w_kt = _rearranged_w_in(w_in[0])
    assert w_cat.shape[1] == _C_END
    g_mix = norm_mix[0].reshape(1, d)

    meta_block = jnp.concatenate([jnp.zeros((PAD, d), x.dtype), meta_tokens.astype(x.dtype)], axis=0)
    real = _in_proj(x2d, g_mix, w_cat, w_kt, _rope_tables(jnp.arange(seq) + N_META), ROW_TILE_PROJ)
    meta = _in_proj(meta_block, g_mix, w_cat, w_kt, _rope_tables(jnp.arange(BLOCK) - PAD), BLOCK)

    j = jnp.arange(2 * BLOCK)[:, None] % BLOCK
    s = jnp.arange(2 * BLOCK)[None, :]
    tri = jnp.where((s >= BLOCK) | (j >= s), -1.0, 0.0).astype(BF16)

    mixed = _attention(sinks[0], real, meta, norm_out_a[0].reshape(1, WIDTH_A),
                       norm_out_b[0].reshape(1, WIDTH_B), tri, batch, seq)

    out = _out_ffn(mixed.reshape(batch * seq, WIDTH_A + WIDTH_B), x2d, w_out[0].astype(BF16),
                   norm_ffn[0].reshape(1, d), w_gate[0].astype(BF16), w_up[0].astype(BF16),
                   w_down[0].astype(BF16), norm_final.reshape(1, d), ROW_TILE_FFN)
    return out.reshape(batch, seq, d)
```

```python
import functools

import jax
import jax.numpy as jnp
from jax import lax
from jax.experimental import pallas as pl
from jax.experimental.pallas import tpu as pltpu

HEAD_DIM = 64
N_HEADS_A = 8
N_KV_A = 2
N_HEADS_B = 8
WIDTH_A = N_HEADS_A * HEAD_DIM
KV_WIDTH_A = N_KV_A * HEAD_DIM
WIDTH_B = N_HEADS_B * HEAD_DIM
BLOCK = 128
N_META = 16
PAD = BLOCK - N_META
ROPE_THETA = 10000.0
EPS = 1e-6
LANES = 128
N_PAIRS_A = WIDTH_A // LANES
N_PAIRS_B = WIDTH_B // LANES
KV_DUP_A = 2 * KV_WIDTH_A

Q_TILE = 2 * BLOCK
STRAIGHT_FULL = 2
ROW_TILE_PROJ = 1024
PROJ_SUB_ROWS = 256
ROW_TILE_FFN = 1024
FFN_SUB_TILES = 4
VMEM_LIMIT = 56 * 1024 * 1024

F32 = jnp.float32
BF16 = jnp.bfloat16
LOG2_E = 1.4426950408889634
DEAD_LOG2 = -150.0

_C_QA = 0
_C_VA = _C_QA + WIDTH_A
_C_QB = _C_VA + KV_DUP_A
_C_VB = _C_QB + WIDTH_B
_C_END = _C_VB + WIDTH_B


def _rms(xf, g):
    ms = jnp.mean(xf * xf, axis=-1, keepdims=True)
    return xf * lax.rsqrt(ms + EPS) * g


def _dot(a, b):
    return jnp.dot(a, b, preferred_element_type=F32)


def _dot_nt(a, b):
    return lax.dot_general(a, b, (((1,), (1,)), ((), ())), preferred_element_type=F32)


def _in_proj_kernel(sub_rows, *refs):
    for r0 in range(0, refs[0].shape[0], sub_rows):
        _in_proj_rows(slice(r0, r0 + sub_rows), r0 // BLOCK, *refs)


def _in_proj_rows(rows, kb0, x_ref, g_ref, w_ref, wkt_ref, cos_ref, sin_ref, cosf_ref, sinf_ref,
                  qa_ref, katlo_ref, kathi_ref, valo_ref, vahi_ref,
                  qb_ref, ktlo_ref, kthi_ref, vblo_ref, vbhi_ref):
    tm = rows.stop - rows.start
    u = _rms(x_ref[rows, :], g_ref[...]).astype(BF16)
    proj = _dot(u, w_ref[...])
    cos = cos_ref[rows, :]
    sin = sin_ref[rows, :]
    lane = lax.broadcasted_iota(jnp.int32, (tm, LANES), 1)
    first_half = (lane & (HEAD_DIM // 2)) == 0
    lower_head = lane < HEAD_DIM
    scale = HEAD_DIM ** -0.5

    def rope(t):
        rot = jnp.where(first_half,
                        pltpu.roll(t, LANES - HEAD_DIM // 2, 1),
                        pltpu.roll(t, HEAD_DIM // 2, 1))
        return t * cos + rot * sin

    def group(c0, j):
        return proj[:, c0 + j * LANES:c0 + (j + 1) * LANES]

    for j in range(N_PAIRS_A):
        sl = slice(j * LANES, (j + 1) * LANES)
        qa_ref[rows, sl] = (rope(group(_C_QA, j)) * scale).astype(BF16)
    for j in range(KV_DUP_A // LANES):
        sl = slice(j * LANES, (j + 1) * LANES)
        v = group(_C_VA, j)
        valo_ref[rows, sl] = jnp.where(lower_head, v, 0.0).astype(BF16)
        vahi_ref[rows, sl] = jnp.where(lower_head, 0.0, v).astype(BF16)
    for j in range(N_PAIRS_B):
        sl = slice(j * LANES, (j + 1) * LANES)
        qb_ref[rows, sl] = (group(_C_QB, j) * (scale * LOG2_E)).astype(BF16)
        v = group(_C_VB, j)
        vblo_ref[rows, sl] = jnp.where(lower_head, v, 0.0).astype(BF16)
        vbhi_ref[rows, sl] = jnp.where(lower_head, 0.0, v).astype(BF16)

    kt_all = _dot_nt(wkt_ref[...], u)

    cos_f = cosf_ref[:, rows]
    sin_f = sinf_ref[:, rows]
    half = HEAD_DIM // 2
    zeros = jnp.zeros((HEAD_DIM, tm), F32)
    lo_parts, hi_parts = [], []
    for c in range(N_KV_A):
        x1 = kt_all[c * HEAD_DIM:c * HEAD_DIM + half]
        x2 = kt_all[c * HEAD_DIM + half:(c + 1) * HEAD_DIM]
        roped = jnp.concatenate([x1 * cos_f - x2 * sin_f, x2 * cos_f + x1 * sin_f], axis=0)
        lo_parts += [roped, zeros]
        hi_parts += [zeros, roped]
    kat_lo = jnp.concatenate(lo_parts, axis=0).astype(BF16)
    kat_hi = jnp.concatenate(hi_parts, axis=0).astype(BF16)

    kt = kt_all[KV_WIDTH_A:]
    lower_feat = (lax.broadcasted_iota(jnp.int32, kt.shape, 0) & HEAD_DIM) == 0
    kt_lo = jnp.where(lower_feat, kt, 0.0).astype(BF16)
    kt_hi = jnp.where(lower_feat, 0.0, kt).astype(BF16)
    for j in range(tm // BLOCK):
        cols = slice(j * BLOCK, (j + 1) * BLOCK)
        katlo_ref[kb0 + j] = kat_lo[:, cols]
        kathi_ref[kb0 + j] = kat_hi[:, cols]
        ktlo_ref[kb0 + j] = kt_lo[:, cols]
        kthi_ref[kb0 + j] = kt_hi[:, cols]


def _in_proj(x2d, gain, w_cat, w_kt, tables, row_tile):
    n, d = x2d.shape
    cos_t, sin_t, cos_f, sin_f = tables
    table_tiles = cos_t.shape[0] // row_tile
    kb_per_tile = row_tile // BLOCK
    row_spec = lambda w: pl.BlockSpec((row_tile, w), lambda i: (i, 0))
    const_spec = lambda shape: pl.BlockSpec(shape, lambda i: (0, 0), pipeline_mode=pl.Buffered(1))
    table_spec = pl.BlockSpec((row_tile, LANES), lambda i: (i % table_tiles, 0))
    ftable_spec = pl.BlockSpec((HEAD_DIM // 2, row_tile), lambda i: (0, i % table_tiles))
    row_out = lambda w: (row_spec(w), jax.ShapeDtypeStruct((n, w), BF16))
    kt_out = lambda w: (pl.BlockSpec((kb_per_tile, w, BLOCK), lambda i: (i, 0, 0)),
                        jax.ShapeDtypeStruct((n // BLOCK, w, BLOCK), BF16))
    outs = [row_out(WIDTH_A), kt_out(KV_DUP_A), kt_out(KV_DUP_A), row_out(KV_DUP_A), row_out(KV_DUP_A),
            row_out(WIDTH_B), kt_out(WIDTH_B), kt_out(WIDTH_B), row_out(WIDTH_B), row_out(WIDTH_B)]
    return pl.pallas_call(
        functools.partial(_in_proj_kernel, min(row_tile, PROJ_SUB_ROWS)),
        grid=(n // row_tile,),
        in_specs=[row_spec(d), const_spec((1, d)), const_spec(w_cat.shape), const_spec(w_kt.shape),
                  table_spec, table_spec, ftable_spec, ftable_spec],
        out_specs=[o[0] for o in outs],
        out_shape=[o[1] for o in outs],
        compiler_params=pltpu.CompilerParams(
            dimension_semantics=("arbitrary",), vmem_limit_bytes=VMEM_LIMIT),
        name="in_proj",
    )(x2d, gain, w_cat, w_kt, cos_t, sin_t, cos_f, sin_f)


def _attention_kernel(sinks_ref, qa_ref, katlo_ref, kathi_ref, valo_ref, vahi_ref,
                      qb_ref, ktlo_ref, kthi_ref, vblo_ref, vbhi_ref,
                      mwa_ref, mva_ref, onesa_ref, mktlo_ref, mkthi_ref, mvblo_ref, mvbhi_ref,
                      ga_ref, gb_ref, tri_ref, out_ref, carry_ref, oacc_ref, zbuf_ref):
    qi = pl.program_id(1)
    blocks_per_tile = Q_TILE // BLOCK
    assert blocks_per_tile == 2, "the group-B sweep peels exactly two diagonal key blocks"
    assert STRAIGHT_FULL == blocks_per_tile, "a tile has either no full key block or >= STRAIGHT_FULL"

    in_cur = ((lax.broadcasted_iota(jnp.int32, (BLOCK, 2 * BLOCK), 1) & (BLOCK - 1))
              <= lax.broadcasted_iota(jnp.int32, (BLOCK, 2 * BLOCK), 0))
    in_cur1 = in_cur[:, :BLOCK]
    lane = lax.broadcasted_iota(jnp.int32, (BLOCK, LANES), 1)
    meta_lo = lane < N_META
    meta_any = lane < 2 * N_META
    lower_lanes = lane < HEAD_DIM
    neg_inf = -jnp.inf
    chains = [(half, g) for half in range(blocks_per_tile) for g in range(N_PAIRS_A)]

    def a_blocks(half):
        blk = qi * blocks_per_tile + half
        return blk, jnp.maximum(blk - 1, 0)

    scores = []
    for half, g in chains:
        blk, prev = a_blocks(half)
        c = (2 * g) // (N_HEADS_A // N_KV_A)
        csl = slice(c * LANES, (c + 1) * LANES)
        q2 = qa_ref[0, half * BLOCK:(half + 1) * BLOCK, g * LANES:(g + 1) * LANES]
        s_cur = _dot(q2, jnp.concatenate([katlo_ref[0, blk, csl, :], kathi_ref[0, blk, csl, :]], axis=1))
        s_prev = _dot(q2, jnp.concatenate([katlo_ref[0, prev, csl, :], kathi_ref[0, prev, csl, :]], axis=1))
        scores.append((s_cur, s_prev, _dot(q2, mwa_ref[c])))
    probs = []
    for (half, g), (s_cur, s_prev, sm) in zip(chains, scores):
        blk, _ = a_blocks(half)
        s = jnp.where(in_cur, s_cur, jnp.where(blk >= 1, s_prev, neg_inf))
        s_lo, s_hi = s[:, :BLOCK], s[:, BLOCK:]
        sink_lo, sink_hi = sinks_ref[2 * g], sinks_ref[2 * g + 1]
        m_lo = jnp.maximum(jnp.max(jnp.maximum(s_lo, jnp.where(meta_lo, sm, neg_inf)),
                                   axis=-1, keepdims=True), sink_lo)
        m_hi = jnp.maximum(jnp.max(jnp.maximum(s_hi, jnp.where(meta_lo | ~meta_any, neg_inf, sm)),
                                   axis=-1, keepdims=True), sink_hi)
        p_lo = jnp.exp(s_lo - m_lo)
        p_hi = jnp.exp(s_hi - m_hi)
        pm = jnp.where(meta_any, jnp.exp(sm - jnp.where(meta_lo, m_lo, m_hi)), 0.0)
        pcat = jnp.concatenate([jnp.where(in_cur1, 0.0, p_lo), jnp.where(in_cur1, 0.0, p_hi),
                                jnp.where(in_cur1, p_lo, 0.0), jnp.where(in_cur1, p_hi, 0.0)],
                               axis=1).astype(BF16)
        sink_term = jnp.where(lower_lanes, jnp.exp(sink_lo - m_lo), jnp.exp(sink_hi - m_hi))
        probs.append((pcat, pm.astype(BF16), sink_term))
    ones_a = onesa_ref[...]
    for half in range(blocks_per_tile):
        blk, prev = a_blocks(half)
        cur0 = pl.multiple_of(blk * BLOCK, BLOCK)
        prev0 = pl.multiple_of(prev * BLOCK, BLOCK)
        oa = []
        for g in range(N_PAIRS_A):
            c = (2 * g) // (N_HEADS_A // N_KV_A)
            csl = slice(c * LANES, (c + 1) * LANES)
            pcat, pm, sink_term = probs[half * N_PAIRS_A + g]
            vcat = jnp.concatenate(
                [valo_ref[0, pl.ds(prev0, BLOCK), csl], vahi_ref[0, pl.ds(prev0, BLOCK), csl],
                 valo_ref[0, pl.ds(cur0, BLOCK), csl], vahi_ref[0, pl.ds(cur0, BLOCK), csl]], axis=0)
            res = _dot(pcat, jnp.concatenate([vcat, ones_a], axis=1)) + _dot(pm, mva_ref[c])
            oa.append(res[:, :LANES] / (res[:, LANES:] + sink_term))
        oa = jnp.concatenate(oa, axis=1)
        out_ref[0, half * BLOCK:(half + 1) * BLOCK, :WIDTH_A] = _rms(oa, ga_ref[...]).astype(out_ref.dtype)

    carry_ref[...] = jnp.zeros_like(carry_ref)
    oacc_ref[...] = jnp.zeros_like(oacc_ref)
    tri = tri_ref[...]
    k_minus_q = (lax.broadcasted_iota(jnp.int32, (Q_TILE, LANES), 1)
                 - lax.broadcasted_iota(jnp.int32, (Q_TILE, LANES), 0))
    meta_valid = lax.broadcasted_iota(jnp.int32, (Q_TILE, LANES), 1) >= PAD

    def pair_lanes(g):
        return slice(g * LANES, (g + 1) * LANES)

    def pair_scores(g, rows, load_kt):
        return _dot(qb_ref[0, rows, pair_lanes(g)], jnp.concatenate(load_kt(pair_lanes(g)), axis=1))

    def scores(rows, load_kt):
        return [pair_scores(g, rows, load_kt) for g in range(N_PAIRS_B)]

    def b_step(zs, load_v, valid, rows=slice(0, Q_TILE), want_carry_max=False, ahead=None):
        rs = []
        for g in range(N_PAIRS_B):
            z = zs[g]
            sp = jnp.maximum(z, 0.0) + jnp.log2(1.0 + jnp.exp2(-jnp.abs(z)))
            r_heads = []
            for h in range(2):
                sp_h = sp[:, h * BLOCK:(h + 1) * BLOCK]
                if valid is not None:
                    sp_h = jnp.where(valid, sp_h, 0.0)
                hi = sp_h.astype(BF16)
                lo = (sp_h - hi.astype(F32)).astype(BF16)
                r_heads.append(_dot(jnp.concatenate([hi, lo], axis=1), tri))
            rs.append(r_heads)
            if ahead is not None:
                slot, load_kt = ahead
                zbuf_ref[slot, g] = pair_scores(g, slice(0, Q_TILE), load_kt)
        carry_max = None
        for g in range(N_PAIRS_B):
            a_heads = []
            for h in range(2):
                r = rs[g][h]
                carry = carry_ref[g, h, rows]
                a = jnp.exp2(zs[g][:, h * BLOCK:(h + 1) * BLOCK] + r[:, :BLOCK] + carry)
                if valid is not None:
                    a = jnp.where(valid, a, 0.0)
                carry = carry + r[:, BLOCK:]
                carry_ref[g, h, rows] = carry
                if want_carry_max:
                    carry_max = carry if carry_max is None else jnp.maximum(carry_max, carry)
                a_heads.append(a.astype(BF16))
            oacc_ref[rows, pair_lanes(g)] += _dot(jnp.concatenate(a_heads, axis=1),
                                                  jnp.concatenate(load_v(pair_lanes(g)), axis=0))
        return jnp.max(carry_max) if want_carry_max else None

    def real_kt(kb):
        return lambda sl: (ktlo_ref[0, kb, sl, :], kthi_ref[0, kb, sl, :])

    def real_v(kb):
        r0 = pl.multiple_of(kb * BLOCK, BLOCK)
        return lambda sl: (vblo_ref[0, pl.ds(r0, BLOCK), sl], vbhi_ref[0, pl.ds(r0, BLOCK), sl])

    def stashed(slot):
        return [zbuf_ref[slot, g] for g in range(N_PAIRS_B)]

    all_rows = slice(0, Q_TILE)
    upper_rows = slice(BLOCK, Q_TILE)
    n_full = qi * blocks_per_tile
    top = n_full + 1
    b_step(scores(upper_rows, real_kt(top)), real_v(top), k_minus_q[upper_rows] < -BLOCK,
           rows=upper_rows, ahead=(0, real_kt(top - 1)))
    b_step(stashed(0), real_v(top - 1), k_minus_q < 0,
           ahead=(1, real_kt(jnp.maximum(n_full - 1, 0))))

    def full_step(j, valid):
        kb = jnp.maximum(n_full - 1 - j, 0)
        slot = (j + 1) & 1
        return b_step(stashed(slot), real_v(kb), valid, want_carry_max=True,
                      ahead=(1 - slot, real_kt(jnp.maximum(kb - 1, 0))))

    exists = k_minus_q < jnp.where(n_full >= STRAIGHT_FULL, Q_TILE + BLOCK, -(Q_TILE + BLOCK))
    for j in range(STRAIGHT_FULL):
        carry_max = full_step(j, exists)

    def full_cond(state):
        j, live = state
        return jnp.logical_and(j < n_full, live > 0)

    def full_body(state):
        j, _ = state
        return j + 1, (full_step(j, None) >= DEAD_LOG2).astype(jnp.int32)

    _, live = lax.while_loop(full_cond, full_body,
                             (jnp.int32(STRAIGHT_FULL), (carry_max >= DEAD_LOG2).astype(jnp.int32)))

    @pl.when(live > 0)
    def _():
        b_step(scores(all_rows, lambda sl: (mktlo_ref[0, sl, :], mkthi_ref[0, sl, :])),
               lambda sl: (mvblo_ref[:, sl], mvbhi_ref[:, sl]), meta_valid)

    out_ref[0, :, WIDTH_A:] = _rms(oacc_ref[...], gb_ref[...]).astype(out_ref.dtype)


def _attention(sinks, real, meta, gain_a, gain_b, tri, batch, seq):
    nq = seq // Q_TILE
    nkb = seq // BLOCK
    qa, katlo, kathi, valo, vahi, qb, ktlo, kthi, vblo, vbhi = real
    rows3 = lambda t: t.reshape(batch, seq, t.shape[-1])
    qa, valo, vahi, qb, vblo, vbhi = [rows3(t) for t in (qa, valo, vahi, qb, vblo, vbhi)]
    slabs = lambda t: t.reshape(batch, nkb, t.shape[-2], BLOCK)
    katlo, kathi, ktlo, kthi = [slabs(t) for t in (katlo, kathi, ktlo, kthi)]
    _, mkatlo, mkathi, mvalo, mvahi, _, mktlo, mkthi, mvblo, mvbhi = meta
    pairs = range(KV_DUP_A // LANES)
    pair = lambda c: slice(c * LANES, (c + 1) * LANES)
    mwa = jnp.stack([jnp.pad(jnp.concatenate([mkatlo[0, pair(c), PAD:], mkathi[0, pair(c), PAD:]], axis=1),
                             ((0, 0), (0, LANES - 2 * N_META))) for c in pairs])
    mva = jnp.stack([jnp.pad(jnp.concatenate([mvalo[PAD:, pair(c)], mvahi[PAD:, pair(c)]], axis=0),
                             ((0, LANES - 2 * N_META), (0, 0))) for c in pairs])
    lower_lane = jnp.arange(LANES)[None, :] < HEAD_DIM
    head_ones = lambda upper_row: (lower_lane != upper_row[:, None]).astype(BF16)
    ones_a = head_ones((jnp.arange(4 * BLOCK) // BLOCK) % 2 == 1)
    meta_row = jnp.arange(LANES)
    meta_ones = jnp.where((meta_row < 2 * N_META)[:, None], head_ones(meta_row >= N_META), 0).astype(BF16)
    mva = jnp.concatenate([mva, jnp.broadcast_to(meta_ones, mva.shape)], axis=2)
    qspec = lambda w: pl.BlockSpec((1, Q_TILE, w), lambda b, i: (b, i, 0))
    seqspec = lambda w: pl.BlockSpec((1, seq, w), lambda b, i: (b, 0, 0))
    ktspec = lambda w: pl.BlockSpec((1, nkb, w, BLOCK), lambda b, i: (b, 0, 0, 0))
    const = lambda a: pl.BlockSpec(a.shape, lambda b, i: (0,) * a.ndim)
    mix = WIDTH_A + WIDTH_B
    return pl.pallas_call(
        _attention_kernel,
        grid=(batch, nq),
        in_specs=[pl.BlockSpec(memory_space=pltpu.SMEM),
                  qspec(WIDTH_A), ktspec(KV_DUP_A), ktspec(KV_DUP_A), seqspec(KV_DUP_A), seqspec(KV_DUP_A),
                  qspec(WIDTH_B), ktspec(WIDTH_B), ktspec(WIDTH_B), seqspec(WIDTH_B), seqspec(WIDTH_B),
                  const(mwa), const(mva), const(ones_a), const(mktlo), const(mkthi),
                  const(mvblo), const(mvbhi), const(gain_a), const(gain_b), const(tri)],
        out_specs=pl.BlockSpec((1, Q_TILE, mix), lambda b, i: (b, i, 0)),
        out_shape=jax.ShapeDtypeStruct((batch, seq, mix), BF16),
        scratch_shapes=[pltpu.VMEM((N_PAIRS_B, 2, Q_TILE, LANES), F32),
                        pltpu.VMEM((Q_TILE, WIDTH_B), F32),
                        pltpu.VMEM((2, N_PAIRS_B, Q_TILE, 2 * BLOCK), F32)],
        compiler_params=pltpu.CompilerParams(
            dimension_semantics=("arbitrary", "arbitrary"), vmem_limit_bytes=VMEM_LIMIT),
        name="attention",
    )(sinks, qa, katlo, kathi, valo, vahi, qb, ktlo, kthi, vblo, vbhi, mwa, mva, ones_a, mktlo, mkthi,
      mvblo, mvbhi, gain_a, gain_b, tri)


def _out_ffn_kernel(ff_chunks, n_sub, mixed_ref, x_ref, wo_ref, gf_ref, wg_ref, wu_ref, wd_ref, gl_ref,
                    out_ref):
    sub = x_ref.shape[0] // n_sub
    rows = [slice(i * sub, (i + 1) * sub) for i in range(n_sub)]
    hs = [x_ref[r, :] + _dot(mixed_ref[r, :], wo_ref[...]) for r in rows]
    for r, h in zip(rows, hs):
        u = _rms(h, gf_ref[...]).astype(BF16)
        y = h
        for c0, c1 in ff_chunks:
            gate = _dot(u, wg_ref[:, c0:c1])
            up = _dot(u, wu_ref[:, c0:c1])
            act = (gate * jax.nn.sigmoid(gate) * up).astype(BF16)
            y = y + _dot(act, wd_ref[c0:c1, :])
        out_ref[r, :] = _rms(y, gl_ref[...])


def _ff_chunks(d_ff, n_chunks, col_tile):
    tiles = -(-d_ff // col_tile)
    bounds = [min(d_ff, (tiles * k // n_chunks) * col_tile) for k in range(n_chunks + 1)]
    return tuple((bounds[k], bounds[k + 1]) for k in range(n_chunks))


def _out_ffn(mixed2d, x2d, w_out, g_ffn, w_gate, w_up, w_down, g_final, row_tile):
    n, d = x2d.shape
    d_ff = w_gate.shape[1]
    row_spec = lambda w: pl.BlockSpec((row_tile, w), lambda i: (i, 0))
    const = lambda a: pl.BlockSpec(a.shape, lambda i: (0, 0), pipeline_mode=pl.Buffered(1))
    mxu_cols = 256
    return pl.pallas_call(
        functools.partial(_out_ffn_kernel, _ff_chunks(d_ff, 2, mxu_cols), FFN_SUB_TILES),
        grid=(n // row_tile,),
        in_specs=[row_spec(mixed2d.shape[1]), row_spec(d), const(w_out), const(g_ffn),
                  const(w_gate), const(w_up), const(w_down), const(g_final)],
        out_specs=row_spec(d),
        out_shape=jax.ShapeDtypeStruct((n, d), x2d.dtype),
        compiler_params=pltpu.CompilerParams(
            dimension_semantics=("arbitrary",), vmem_limit_bytes=VMEM_LIMIT),
        name="out_ffn",
    )(mixed2d, x2d, w_out, g_ffn, w_gate, w_up, w_down, g_final)


def _rope_tables(pos):
    half = HEAD_DIM // 2
    inv_freq = ROPE_THETA ** (-jnp.arange(half, dtype=F32) / half)
    ang = pos.astype(F32)[:, None] * inv_freq[None, :]
    cos, sin = jnp.cos(ang), jnp.sin(ang)
    reps = LANES // HEAD_DIM
    return (jnp.tile(jnp.concatenate([cos, cos], axis=1), (1, reps)),
            jnp.tile(jnp.concatenate([-sin, sin], axis=1), (1, reps)), cos.T, sin.T)


def _rearranged_w_in(w):
    o_ka = WIDTH_A
    o_va = o_ka + KV_WIDTH_A
    o_qb = o_va + KV_WIDTH_A
    o_kb = o_qb + WIDTH_B
    o_vb = o_kb + WIDTH_B
    dup = lambda t: jnp.repeat(t.reshape(t.shape[0], N_KV_A, 1, HEAD_DIM), 2, axis=2).reshape(t.shape[0], KV_DUP_A)
    w_cat = jnp.concatenate([w[:, :o_ka], dup(w[:, o_va:o_qb]), w[:, o_qb:o_kb], w[:, o_vb:]],
                            axis=1).astype(BF16)
    w_kt = jnp.concatenate([w[:, o_ka:o_va], w[:, o_kb:o_vb]], axis=1).T.astype(BF16)
    return w_cat, w_kt


def kernel(x, meta_tokens, norm_mix, w_in, sinks, norm_out_a, norm_out_b, w_out,
           norm_ffn, w_gate, w_up, w_down, norm_final):
    batch, seq, d = x.shape
    assert norm_mix.shape[0] == 1, "single-layer block: meta rows are not carried past the mixer"
    assert seq % ROW_TILE_PROJ == 0 and (batch * seq) % ROW_TILE_FFN == 0 and seq % Q_TILE == 0
    x2d = x.reshape(batch * seq, d)
    w_cat, w_kt = _rearranged_w_in(w_in[0])
    assert w_cat.shape[1] == _C_END
    g_mix = norm_mix[0].reshape(1, d)

    meta_block = jnp.concatenate([jnp.zeros((PAD, d), x.dtype), meta_tokens.astype(x.dtype)], axis=0)
    real = _in_proj(x2d, g_mix, w_cat, w_kt, _rope_tables(jnp.arange(seq) + N_META), ROW_TILE_PROJ)
    meta = _in_proj(meta_block, g_mix, w_cat, w_kt, _rope_tables(jnp.arange(BLOCK) - PAD), BLOCK)

    j = jnp.arange(2 * BLOCK)[:, None] % BLOCK
    s = jnp.arange(2 * BLOCK)[None, :]
    tri = jnp.where((s >= BLOCK) | (j >= s), -1.0, 0.0).astype(BF16)

    mixed = _attention(sinks[0], real, meta, norm_out_a[0].reshape(1, WIDTH_A),
                       norm_out_b[0].reshape(1, WIDTH_B), tri, batch, seq)

    out = _out_ffn(mixed.reshape(batch * seq, WIDTH_A + WIDTH_B), x2d, w_out[0].astype(BF16),
                   norm_ffn[0].reshape(1, d), w_gate[0].astype(BF16), w_up[0].astype(BF16),
                   w_down[0].astype(BF16), norm_final.reshape(1, d), ROW_TILE_FFN)
    return out.reshape(batch, seq, d)
```

```python
import functools

import jax
import jax.numpy as jnp
from jax import lax
from jax.experimental import pallas as pl
from jax.experimental.pallas import tpu as pltpu

HEAD_DIM = 64
N_HEADS_A = 8
N_KV_A = 2
N_HEADS_B = 8
WIDTH_A = N_HEADS_A * HEAD_DIM
KV_WIDTH_A = N_KV_A * HEAD_DIM
WIDTH_B = N_HEADS_B * HEAD_DIM
BLOCK = 128
N_META = 16
PAD = BLOCK - N_META
ROPE_THETA = 10000.0
EPS = 1e-6
LANES = 128
N_PAIRS_A = WIDTH_A // LANES
N_PAIRS_B = WIDTH_B // LANES
KV_DUP_A = 2 * KV_WIDTH_A

Q_TILE = 2 * BLOCK
STRAIGHT_FULL = 2
ROW_TILE_PROJ = 1024
PROJ_SUB_ROWS = 256
ROW_TILE_FFN = 1024
FFN_SUB_TILES = 4
VMEM_LIMIT = 56 * 1024 * 1024

F32 = jnp.float32
BF16 = jnp.bfloat16
LOG2_E = 1.4426950408889634
DEAD_LOG2 = -150.0

_C_QA = 0
_C_VA = _C_QA + WIDTH_A
_C_QB = _C_VA + KV_DUP_A
_C_VB = _C_QB + WIDTH_B
_C_END = _C_VB + WIDTH_B


def _rms(xf, g):
    ms = jnp.mean(xf * xf, axis=-1, keepdims=True)
    return xf * lax.rsqrt(ms + EPS) * g


def _dot(a, b):
    return jnp.dot(a, b, preferred_element_type=F32)


def _dot_nt(a, b):
    return lax.dot_general(a, b, (((1,), (1,)), ((), ())), preferred_element_type=F32)


def _in_proj_kernel(sub_rows, *refs):
    for r0 in range(0, refs[0].shape[0], sub_rows):
        _in_proj_rows(slice(r0, r0 + sub_rows), r0 // BLOCK, *refs)


def _in_proj_rows(rows, kb0, x_ref, g_ref, w_ref, wkt_ref, cos_ref, sin_ref, cosf_ref, sinf_ref,
                  qa_ref, katlo_ref, kathi_ref, valo_ref, vahi_ref,
                  qb_ref, ktlo_ref, kthi_ref, vblo_ref, vbhi_ref):
    tm = rows.stop - rows.start
    u = _rms(x_ref[rows, :], g_ref[...]).astype(BF16)
    proj = _dot(u, w_ref[...])
    cos = cos_ref[rows, :]
    sin = sin_ref[rows, :]
    lane = lax.broadcasted_iota(jnp.int32, (tm, LANES), 1)
    first_half = (lane & (HEAD_DIM // 2)) == 0
    lower_head = lane < HEAD_DIM
    scale = HEAD_DIM ** -0.5

    def rope(t):
        rot = jnp.where(first_half,
                        pltpu.roll(t, LANES - HEAD_DIM // 2, 1),
                        pltpu.roll(t, HEAD_DIM // 2, 1))
        return t * cos + rot * sin

    def group(c0, j):
        return proj[:, c0 + j * LANES:c0 + (j + 1) * LANES]

    for j in range(N_PAIRS_A):
        sl = slice(j * LANES, (j + 1) * LANES)
        qa_ref[rows, sl] = (rope(group(_C_QA, j)) * scale).astype(BF16)
    for j in range(KV_DUP_A // LANES):
        sl = slice(j * LANES, (j + 1) * LANES)
        v = group(_C_VA, j)
        valo_ref[rows, sl] = jnp.where(lower_head, v, 0.0).astype(BF16)
        vahi_ref[rows, sl] = jnp.where(lower_head, 0.0, v).astype(BF16)
    for j in range(N_PAIRS_B):
        sl = slice(j * LANES, (j + 1) * LANES)
        qb_ref[rows, sl] = (group(_C_QB, j) * (scale * LOG2_E)).astype(BF16)
        v = group(_C_VB, j)
        vblo_ref[rows, sl] = jnp.where(lower_head, v, 0.0).astype(BF16)
        vbhi_ref[rows, sl] = jnp.where(lower_head, 0.0, v).astype(BF16)

    kt_all = _dot_nt(wkt_ref[...], u)

    cos_f = cosf_ref[:, rows]
    sin_f = sinf_ref[:, rows]
    half = HEAD_DIM // 2
    zeros = jnp.zeros((HEAD_DIM, tm), F32)
    lo_parts, hi_parts = [], []
    for c in range(N_KV_A):
        x1 = kt_all[c * HEAD_DIM:c * HEAD_DIM + half]
        x2 = kt_all[c * HEAD_DIM + half:(c + 1) * HEAD_DIM]
        roped = jnp.concatenate([x1 * cos_f - x2 * sin_f, x2 * cos_f + x1 * sin_f], axis=0)
        lo_parts += [roped, zeros]
        hi_parts += [zeros, roped]
    kat_lo = jnp.concatenate(lo_parts, axis=0).astype(BF16)
    kat_hi = jnp.concatenate(hi_parts, axis=0).astype(BF16)

    kt = kt_all[KV_WIDTH_A:]
    lower_feat = (lax.broadcasted_iota(jnp.int32, kt.shape, 0) & HEAD_DIM) == 0
    kt_lo = jnp.where(lower_feat, kt, 0.0).astype(BF16)
    kt_hi = jnp.where(lower_feat, 0.0, kt).astype(BF16)
    for j in range(tm // BLOCK):
        cols = slice(j * BLOCK, (j + 1) * BLOCK)
        katlo_ref[kb0 + j] = kat_lo[:, cols]
        kathi_ref[kb0 + j] = kat_hi[:, cols]
        ktlo_ref[kb0 + j] = kt_lo[:, cols]
        kthi_ref[kb0 + j] = kt_hi[:, cols]


def _in_proj(x2d, gain, w_cat, w_kt, tables, row_tile):
    n, d = x2d.shape
    cos_t, sin_t, cos_f, sin_f = tables
    table_tiles = cos_t.shape[0] // row_tile
    kb_per_tile = row_tile // BLOCK
    row_spec = lambda w: pl.BlockSpec((row_tile, w), lambda i: (i, 0))
    const_spec = lambda shape: pl.BlockSpec(shape, lambda i: (0, 0), pipeline_mode=pl.Buffered(1))
    table_spec = pl.BlockSpec((row_tile, LANES), lambda i: (i % table_tiles, 0))
    ftable_spec = pl.BlockSpec((HEAD_DIM // 2, row_tile), lambda i: (0, i % table_tiles))
    row_out = lambda w: (row_spec(w), jax.ShapeDtypeStruct((n, w), BF16))
    kt_out = lambda w: (pl.BlockSpec((kb_per_tile, w, BLOCK), lambda i: (i, 0, 0)),
                        jax.ShapeDtypeStruct((n // BLOCK, w, BLOCK), BF16))
    outs = [row_out(WIDTH_A), kt_out(KV_DUP_A), kt_out(KV_DUP_A), row_out(KV_DUP_A), row_out(KV_DUP_A),
            row_out(WIDTH_B), kt_out(WIDTH_B), kt_out(WIDTH_B), row_out(WIDTH_B), row_out(WIDTH_B)]
    return pl.pallas_call(
        functools.partial(_in_proj_kernel, min(row_tile, PROJ_SUB_ROWS)),
        grid=(n // row_tile,),
        in_specs=[row_spec(d), const_spec((1, d)), const_spec(w_cat.shape), const_spec(w_kt.shape),
                  table_spec, table_spec, ftable_spec, ftable_spec],
        out_specs=[o[0] for o in outs],
        out_shape=[o[1] for o in outs],
        compiler_params=pltpu.CompilerParams(
            dimension_semantics=("arbitrary",), vmem_limit_bytes=VMEM_LIMIT),
        name="in_proj",
    )(x2d, gain, w_cat, w_kt, cos_t, sin_t, cos_f, sin_f)


def _attention_kernel(*refs):
    qi = pl.program_id(1)

    @pl.when(qi == 0)
    def _():
        _attention_tile(0, *refs)

    @pl.when(qi > 0)
    def _():
        _attention_tile(qi, *refs)


def _attention_tile(qi, sinks_ref, qa_ref, katlo_ref, kathi_ref, valo_ref, vahi_ref,
                    qb_ref, ktlo_ref, kthi_ref, vblo_ref, vbhi_ref,
                    mwa_ref, mva_ref, onesa_ref, mktlo_ref, mkthi_ref, mvblo_ref, mvbhi_ref,
                    ga_ref, gb_ref, tri_ref, out_ref, carry_ref, oacc_ref, zbuf_ref):
    first_tile = isinstance(qi, int)
    blocks_per_tile = Q_TILE // BLOCK
    assert blocks_per_tile == 2, "the group-B sweep peels exactly two diagonal key blocks"
    assert STRAIGHT_FULL == blocks_per_tile, "every later tile has >= STRAIGHT_FULL full key blocks"

    def row_start(blk):
        return blk * BLOCK if isinstance(blk, int) else pl.multiple_of(blk * BLOCK, BLOCK)

    in_cur = ((lax.broadcasted_iota(jnp.int32, (BLOCK, 2 * BLOCK), 1) & (BLOCK - 1))
              <= lax.broadcasted_iota(jnp.int32, (BLOCK, 2 * BLOCK), 0))
    in_cur1 = in_cur[:, :BLOCK]
    lane = lax.broadcasted_iota(jnp.int32, (BLOCK, LANES), 1)
    meta_lo = lane < N_META
    meta_any = lane < 2 * N_META
    lower_lanes = lane < HEAD_DIM
    neg_inf = -jnp.inf
    chains = [(half, g) for half in range(blocks_per_tile) for g in range(N_PAIRS_A)]

    def a_blocks(half):
        blk = qi * blocks_per_tile + half
        return blk, (None if first_tile and half == 0 else blk - 1)

    def a_keys(blk, csl):
        return jnp.concatenate([katlo_ref[0, blk, csl, :], kathi_ref[0, blk, csl, :]], axis=1)

    scores = []
    for half, g in chains:
        blk, prev = a_blocks(half)
        c = (2 * g) // (N_HEADS_A // N_KV_A)
        csl = slice(c * LANES, (c + 1) * LANES)
        q2 = qa_ref[0, half * BLOCK:(half + 1) * BLOCK, g * LANES:(g + 1) * LANES]
        s_prev = neg_inf if prev is None else _dot(q2, a_keys(prev, csl))
        scores.append((_dot(q2, a_keys(blk, csl)), s_prev, _dot(q2, mwa_ref[c])))
    probs = []
    for (half, g), (s_cur, s_prev, sm) in zip(chains, scores):
        s = jnp.where(in_cur, s_cur, s_prev)
        s_lo, s_hi = s[:, :BLOCK], s[:, BLOCK:]
        sink_lo, sink_hi = sinks_ref[2 * g], sinks_ref[2 * g + 1]
        m_lo = jnp.maximum(jnp.max(jnp.maximum(s_lo, jnp.where(meta_lo, sm, neg_inf)),
                                   axis=-1, keepdims=True), sink_lo)
        m_hi = jnp.maximum(jnp.max(jnp.maximum(s_hi, jnp.where(meta_lo | ~meta_any, neg_inf, sm)),
                                   axis=-1, keepdims=True), sink_hi)
        p_lo = jnp.exp(s_lo - m_lo)
        p_hi = jnp.exp(s_hi - m_hi)
        pm = jnp.where(meta_any, jnp.exp(sm - jnp.where(meta_lo, m_lo, m_hi)), 0.0)
        pcat = jnp.concatenate([jnp.where(in_cur1, 0.0, p_lo), jnp.where(in_cur1, 0.0, p_hi),
                                jnp.where(in_cur1, p_lo, 0.0), jnp.where(in_cur1, p_hi, 0.0)],
                               axis=1).astype(BF16)
        sink_term = jnp.where(lower_lanes, jnp.exp(sink_lo - m_lo), jnp.exp(sink_hi - m_hi))
        probs.append((pcat, pm.astype(BF16), sink_term))
    ones_a = onesa_ref[...]
    for half in range(blocks_per_tile):
        blk, prev = a_blocks(half)
        cur0 = row_start(blk)
        prev0 = cur0 if prev is None else row_start(prev)
        oa = []
        for g in range(N_PAIRS_A):
            c = (2 * g) // (N_HEADS_A // N_KV_A)
            csl = slice(c * LANES, (c + 1) * LANES)
            pcat, pm, sink_term = probs[half * N_PAIRS_A + g]
            vcat = jnp.concatenate(
                [valo_ref[0, pl.ds(prev0, BLOCK), csl], vahi_ref[0, pl.ds(prev0, BLOCK), csl],
                 valo_ref[0, pl.ds(cur0, BLOCK), csl], vahi_ref[0, pl.ds(cur0, BLOCK), csl]], axis=0)
            res = _dot(pcat, jnp.concatenate([vcat, ones_a], axis=1)) + _dot(pm, mva_ref[c])
            oa.append(res[:, :LANES] / (res[:, LANES:] + sink_term))
        oa = jnp.concatenate(oa, axis=1)
        out_ref[0, half * BLOCK:(half + 1) * BLOCK, :WIDTH_A] = _rms(oa, ga_ref[...]).astype(out_ref.dtype)

    carry_ref[...] = jnp.zeros_like(carry_ref)
    oacc_ref[...] = jnp.zeros_like(oacc_ref)
    tri = tri_ref[...]
    k_minus_q = (lax.broadcasted_iota(jnp.int32, (Q_TILE, LANES), 1)
                 - lax.broadcasted_iota(jnp.int32, (Q_TILE, LANES), 0))
    meta_valid = lax.broadcasted_iota(jnp.int32, (Q_TILE, LANES), 1) >= PAD

    def pair_lanes(g):
        return slice(g * LANES, (g + 1) * LANES)

    def pair_scores(g, rows, load_kt):
        return _dot(qb_ref[0, rows, pair_lanes(g)], jnp.concatenate(load_kt(pair_lanes(g)), axis=1))

    def scores(rows, load_kt):
        return [pair_scores(g, rows, load_kt) for g in range(N_PAIRS_B)]

    def masked(t, valid):
        n = valid.shape[0]
        head = jnp.where(valid, t[:n], 0.0)
        return head if n == t.shape[0] else jnp.concatenate([head, t[n:]], axis=0)

    def b_step(zs, load_v, valid, rows=slice(0, Q_TILE), want_carry_max=False, ahead=None):
        rs, row_sums = [], []
        valid2 = None if valid is None else jnp.concatenate([valid, valid], axis=1)
        for g in range(N_PAIRS_B):
            z = zs[g]
            sp = jnp.maximum(z, 0.0) + jnp.log2(1.0 + jnp.exp2(-jnp.abs(z)))
            if valid2 is not None:
                sp = masked(sp, valid2)
            rs.append(_dot(sp.astype(BF16), tri))
            row_sums.append([jnp.sum(sp[:, h * BLOCK:(h + 1) * BLOCK], axis=-1, keepdims=True)
                             for h in range(2)])
            if ahead is not None:
                slot, load_kt = ahead
                zbuf_ref[slot, g] = pair_scores(g, slice(0, Q_TILE), load_kt)
        carry_max = None
        for g in range(N_PAIRS_B):
            a_heads = []
            for h in range(2):
                keys = slice(h * BLOCK, (h + 1) * BLOCK)
                carry = carry_ref[g, h, rows]
                a = jnp.exp2(zs[g][:, keys] + rs[g][:, keys] + carry)
                if valid is not None:
                    a = masked(a, valid)
                carry = carry - row_sums[g][h]
                carry_ref[g, h, rows] = carry
                if want_carry_max:
                    carry_max = carry if carry_max is None else jnp.maximum(carry_max, carry)
                a_heads.append(a.astype(BF16))
            oacc_ref[rows, pair_lanes(g)] += _dot(jnp.concatenate(a_heads, axis=1),
                                                  jnp.concatenate(load_v(pair_lanes(g)), axis=0))
        return jnp.max(carry_max) if want_carry_max else None

    def real_kt(kb):
        return lambda sl: (ktlo_ref[0, kb, sl, :], kthi_ref[0, kb, sl, :])

    def real_v(kb):
        r0 = row_start(kb)
        return lambda sl: (vblo_ref[0, pl.ds(r0, BLOCK), sl], vbhi_ref[0, pl.ds(r0, BLOCK), sl])

    def stashed(slot):
        return [zbuf_ref[slot, g] for g in range(N_PAIRS_B)]

    all_rows = slice(0, Q_TILE)
    upper_rows = slice(BLOCK, Q_TILE)
    n_full = qi * blocks_per_tile
    top = n_full + 1
    meta_kt = lambda sl: (mktlo_ref[0, sl, :], mkthi_ref[0, sl, :])
    meta_v = lambda sl: (mvblo_ref[:, sl], mvbhi_ref[:, sl])
    b_step(scores(upper_rows, real_kt(top)), real_v(top), k_minus_q[upper_rows] < -BLOCK,
           rows=upper_rows, ahead=(0, real_kt(top - 1)))
    b_step(stashed(0), real_v(top - 1), k_minus_q[:BLOCK] < 0,
           ahead=(1, meta_kt if first_tile else real_kt(n_full - 1)))

    if first_tile:
        b_step(stashed(1), meta_v, meta_valid)
    else:
        def full_step(j):
            kb = jnp.maximum(n_full - 1 - j, 0)
            slot = (j + 1) & 1
            return b_step(stashed(slot), real_v(kb), None, want_carry_max=True,
                          ahead=(1 - slot, real_kt(jnp.maximum(kb - 1, 0))))

        for j in range(STRAIGHT_FULL):
            carry_max = full_step(j)

        def full_cond(state):
            j, live = state
            return jnp.logical_and(j < n_full, live > 0)

        def full_body(state):
            j, _ = state
            return j + 1, (full_step(j) >= DEAD_LOG2).astype(jnp.int32)

        _, live = lax.while_loop(full_cond, full_body,
                                 (jnp.int32(STRAIGHT_FULL), (carry_max >= DEAD_LOG2).astype(jnp.int32)))

        @pl.when(live > 0)
        def _():
            b_step(scores(all_rows, meta_kt), meta_v, meta_valid)

    out_ref[0, :, WIDTH_A:] = _rms(oacc_ref[...], gb_ref[...]).astype(out_ref.dtype)


def _attention(sinks, real, meta, gain_a, gain_b, tri, batch, seq):
    nq = seq // Q_TILE
    nkb = seq // BLOCK
    qa, katlo, kathi, valo, vahi, qb, ktlo, kthi, vblo, vbhi = real
    rows3 = lambda t: t.reshape(batch, seq, t.shape[-1])
    qa, valo, vahi, qb, vblo, vbhi = [rows3(t) for t in (qa, valo, vahi, qb, vblo, vbhi)]
    slabs = lambda t: t.reshape(batch, nkb, t.shape[-2], BLOCK)
    katlo, kathi, ktlo, kthi = [slabs(t) for t in (katlo, kathi, ktlo, kthi)]
    _, mkatlo, mkathi, mvalo, mvahi, _, mktlo, mkthi, mvblo, mvbhi = meta
    pairs = range(KV_DUP_A // LANES)
    pair = lambda c: slice(c * LANES, (c + 1) * LANES)
    mwa = jnp.stack([jnp.pad(jnp.concatenate([mkatlo[0, pair(c), PAD:], mkathi[0, pair(c), PAD:]], axis=1),
                             ((0, 0), (0, LANES - 2 * N_META))) for c in pairs])
    mva = jnp.stack([jnp.pad(jnp.concatenate([mvalo[PAD:, pair(c)], mvahi[PAD:, pair(c)]], axis=0),
                             ((0, LANES - 2 * N_META), (0, 0))) for c in pairs])
    lower_lane = jnp.arange(LANES)[None, :] < HEAD_DIM
    head_ones = lambda upper_row: (lower_lane != upper_row[:, None]).astype(BF16)
    ones_a = head_ones((jnp.arange(4 * BLOCK) // BLOCK) % 2 == 1)
    meta_row = jnp.arange(LANES)
    meta_ones = jnp.where((meta_row < 2 * N_META)[:, None], head_ones(meta_row >= N_META), 0).astype(BF16)
    mva = jnp.concatenate([mva, jnp.broadcast_to(meta_ones, mva.shape)], axis=2)
    qspec = lambda w: pl.BlockSpec((1, Q_TILE, w), lambda b, i: (b, i, 0))
    seqspec = lambda w: pl.BlockSpec((1, seq, w), lambda b, i: (b, 0, 0))
    ktspec = lambda w: pl.BlockSpec((1, nkb, w, BLOCK), lambda b, i: (b, 0, 0, 0))
    const = lambda a: pl.BlockSpec(a.shape, lambda b, i: (0,) * a.ndim)
    mix = WIDTH_A + WIDTH_B
    return pl.pallas_call(
        _attention_kernel,
        grid=(batch, nq),
        in_specs=[pl.BlockSpec(memory_space=pltpu.SMEM),
                  qspec(WIDTH_A), ktspec(KV_DUP_A), ktspec(KV_DUP_A), seqspec(KV_DUP_A), seqspec(KV_DUP_A),
                  qspec(WIDTH_B), ktspec(WIDTH_B), ktspec(WIDTH_B), seqspec(WIDTH_B), seqspec(WIDTH_B),
                  const(mwa), const(mva), const(ones_a), const(mktlo), const(mkthi),
                  const(mvblo), const(mvbhi), const(gain_a), const(gain_b), const(tri)],
        out_specs=pl.BlockSpec((1, Q_TILE, mix), lambda b, i: (b, i, 0)),
        out_shape=jax.ShapeDtypeStruct((batch, seq, mix), BF16),
        scratch_shapes=[pltpu.VMEM((N_PAIRS_B, 2, Q_TILE, LANES), F32),
                        pltpu.VMEM((Q_TILE, WIDTH_B), F32),
                        pltpu.VMEM((2, N_PAIRS_B, Q_TILE, 2 * BLOCK), F32)],
        compiler_params=pltpu.CompilerParams(
            dimension_semantics=("arbitrary", "arbitrary"), vmem_limit_bytes=VMEM_LIMIT),
        name="attention",
    )(sinks, qa, katlo, kathi, valo, vahi, qb, ktlo, kthi, vblo, vbhi, mwa, mva, ones_a, mktlo, mkthi,
      mvblo, mvbhi, gain_a, gain_b, tri)


def _out_ffn_kernel(ff_chunks, n_sub, mixed_ref, x_ref, wo_ref, gf_ref, wg_ref, wu_ref, wd_ref, gl_ref,
                    out_ref):
    sub = x_ref.shape[0] // n_sub
    rows = [slice(i * sub, (i + 1) * sub) for i in range(n_sub)]
    hs = [x_ref[r, :] + _dot(mixed_ref[r, :], wo_ref[...]) for r in rows]
    for r, h in zip(rows, hs):
        u = _rms(h, gf_ref[...]).astype(BF16)
        y = h
        for c0, c1 in ff_chunks:
            gate = _dot(u, wg_ref[:, c0:c1])
            up = _dot(u, wu_ref[:, c0:c1])
            act = (gate * jax.nn.sigmoid(gate) * up).astype(BF16)
            y = y + _dot(act, wd_ref[c0:c1, :])
        out_ref[r, :] = _rms(y, gl_ref[...])


def _ff_chunks(d_ff, n_chunks, col_tile):
    tiles = -(-d_ff // col_tile)
    bounds = [min(d_ff, (tiles * k // n_chunks) * col_tile) for k in range(n_chunks + 1)]
    return tuple((bounds[k], bounds[k + 1]) for k in range(n_chunks))


def _out_ffn(mixed2d, x2d, w_out, g_ffn, w_gate, w_up, w_down, g_final, row_tile):
    n, d = x2d.shape
    d_ff = w_gate.shape[1]
    row_spec = lambda w: pl.BlockSpec((row_tile, w), lambda i: (i, 0))
    const = lambda a: pl.BlockSpec(a.shape, lambda i: (0, 0), pipeline_mode=pl.Buffered(1))
    mxu_cols = 256
    return pl.pallas_call(
        functools.partial(_out_ffn_kernel, _ff_chunks(d_ff, 2, mxu_cols), FFN_SUB_TILES),
        grid=(n // row_tile,),
        in_specs=[row_spec(mixed2d.shape[1]), row_spec(d), const(w_out), const(g_ffn),
                  const(w_gate), const(w_up), const(w_down), const(g_final)],
        out_specs=row_spec(d),
        out_shape=jax.ShapeDtypeStruct((n, d), x2d.dtype),
        compiler_params=pltpu.CompilerParams(
            dimension_semantics=("arbitrary",), vmem_limit_bytes=VMEM_LIMIT),
        name="out_ffn",
    )(mixed2d, x2d, w_out, g_ffn, w_gate, w_up, w_down, g_final)


def _rope_tables(pos):
    half = HEAD_DIM // 2
    inv_freq = ROPE_THETA ** (-jnp.arange(half, dtype=F32) / half)
    ang = pos.astype(F32)[:, None] * inv_freq[None, :]
    cos, sin = jnp.cos(ang), jnp.sin(ang)
    reps = LANES // HEAD_DIM
    return (jnp.tile(jnp.concatenate([cos, cos], axis=1), (1, reps)),
            jnp.tile(jnp.concatenate([-sin, sin], axis=1), (1, reps)), cos.T, sin.T)


def _rearranged_w_in(w):
    o_ka = WIDTH_A
    o_va = o_ka + KV_WIDTH_A
    o_qb = o_va + KV_WIDTH_A
    o_kb = o_qb + WIDTH_B
    o_vb = o_kb + WIDTH_B
    dup = lambda t: jnp.repeat(t.reshape(t.shape[0], N_KV_A, 1, HEAD_DIM), 2, axis=2).reshape(t.shape[0], KV_DUP_A)
    w_cat = jnp.concatenate([w[:, :o_ka], dup(w[:, o_va:o_qb]), w[:, o_qb:o_kb], w[:, o_vb:]],
                            axis=1).astype(BF16)
    w_kt = jnp.concatenate([w[:, o_ka:o_va], w[:, o_kb:o_vb]], axis=1).T.astype(BF16)
    return w_cat, w_kt


def kernel(x, meta_tokens, norm_mix, w_in, sinks, norm_out_a, norm_out_b, w_out,
           norm_ffn, w_gate, w_up, w_down, norm_final):
    batch, seq, d = x.shape
    assert norm_mix.shape[0] == 1, "single-layer block: meta rows are not carried past the mixer"
    assert seq % ROW_TILE_PROJ == 0 and (batch * seq) % ROW_TILE_FFN == 0 and seq % Q_TILE == 0
    x2d = x.reshape(batch * seq, d)
    w_cat, w_kt = _rearranged_w_in(w_in[0])
    assert w_cat.shape[1] == _C_END
    g_mix = norm_mix[0].reshape(1, d)

    meta_block = jnp.concatenate([jnp.zeros((PAD, d), x.dtype), meta_tokens.astype(x.dtype)], axis=0)
    real = _in_proj(x2d, g_mix, w_cat, w_kt, _rope_tables(jnp.arange(seq) + N_META), ROW_TILE_PROJ)
    meta = _in_proj(meta_block, g_mix, w_cat, w_kt, _rope_tables(jnp.arange(BLOCK) - PAD), BLOCK)

    j = jnp.arange(2 * BLOCK)[:, None]
    s = jnp.arange(2 * BLOCK)[None, :]
    tri = jnp.where((j // BLOCK == s // BLOCK) & (j >= s), -1.0, 0.0).astype(BF16)

    mixed = _attention(sinks[0], real, meta, norm_out_a[0].reshape(1, WIDTH_A),
                       norm_out_b[0].reshape(1, WIDTH_B), tri, batch, seq)

    out = _out_ffn(mixed.reshape(batch * seq, WIDTH_A + WIDTH_B), x2d, w_out[0].astype(BF16),
                   norm_ffn[0].reshape(1, d), w_gate[0].astype(BF16), w_up[0].astype(BF16),
                   w_down[0].astype(BF16), norm_final.reshape(1, d), ROW_TILE_FFN)
    return out.reshape(batch, seq, d)
```

```python
import functools

import jax
import jax.numpy as jnp
from jax import lax
from jax.experimental import pallas as pl
from jax.experimental.pallas import tpu as pltpu

HEAD_DIM = 64
N_HEADS_A = 8
N_KV_A = 2
N_HEADS_B = 8
WIDTH_A = N_HEADS_A * HEAD_DIM
KV_WIDTH_A = N_KV_A * HEAD_DIM
WIDTH_B = N_HEADS_B * HEAD_DIM
BLOCK = 128
N_META = 16
PAD = BLOCK - N_META
ROPE_THETA = 10000.0
EPS = 1e-6
LANES = 128
N_PAIRS_A = WIDTH_A // LANES
N_PAIRS_B = WIDTH_B // LANES
KV_DUP_A = 2 * KV_WIDTH_A

Q_TILE = 2 * BLOCK
ROW_TILE_PROJ = 1024
PROJ_SUB_ROWS = 256
ROW_TILE_FFN = 1024
FFN_SUB_TILES = 4
VMEM_LIMIT = 56 * 1024 * 1024

F32 = jnp.float32
BF16 = jnp.bfloat16
LOG2_E = 1.4426950408889634
DEAD_LOG2 = -150.0

_C_QA = 0
_C_VA = _C_QA + WIDTH_A
_C_QB = _C_VA + KV_DUP_A
_C_VB = _C_QB + WIDTH_B
_C_END = _C_VB + WIDTH_B


def _rms(xf, g):
    ms = jnp.mean(xf * xf, axis=-1, keepdims=True)
    return xf * lax.rsqrt(ms + EPS) * g


def _dot(a, b):
    return jnp.dot(a, b, preferred_element_type=F32)


def _dot_nt(a, b):
    return lax.dot_general(a, b, (((1,), (1,)), ((), ())), preferred_element_type=F32)


def _in_proj_kernel(sub_rows, *refs):
    for r0 in range(0, refs[0].shape[0], sub_rows):
        _in_proj_rows(slice(r0, r0 + sub_rows), r0 // BLOCK, *refs)


def _in_proj_rows(rows, kb0, x_ref, g_ref, w_ref, wkt_ref, cos_ref, sin_ref, cosf_ref, sinf_ref,
                  qa_ref, katlo_ref, kathi_ref, valo_ref, vahi_ref,
                  qb_ref, ktlo_ref, kthi_ref, vblo_ref, vbhi_ref):
    tm = rows.stop - rows.start
    u = _rms(x_ref[rows, :], g_ref[...]).astype(BF16)
    proj = _dot(u, w_ref[...])
    cos = cos_ref[rows, :]
    sin = sin_ref[rows, :]
    lane = lax.broadcasted_iota(jnp.int32, (tm, LANES), 1)
    first_half = (lane & (HEAD_DIM // 2)) == 0
    lower_head = lane < HEAD_DIM
    scale = HEAD_DIM ** -0.5

    def rope(t):
        rot = jnp.where(first_half,
                        pltpu.roll(t, LANES - HEAD_DIM // 2, 1),
                        pltpu.roll(t, HEAD_DIM // 2, 1))
        return t * cos + rot * sin

    def group(c0, j):
        return proj[:, c0 + j * LANES:c0 + (j + 1) * LANES]

    for j in range(N_PAIRS_A):
        sl = slice(j * LANES, (j + 1) * LANES)
        qa_ref[rows, sl] = (rope(group(_C_QA, j)) * (scale * LOG2_E)).astype(BF16)
    for j in range(KV_DUP_A // LANES):
        sl = slice(j * LANES, (j + 1) * LANES)
        v = group(_C_VA, j)
        valo_ref[rows, sl] = jnp.where(lower_head, v, 0.0).astype(BF16)
        vahi_ref[rows, sl] = jnp.where(lower_head, 0.0, v).astype(BF16)
    for j in range(N_PAIRS_B):
        sl = slice(j * LANES, (j + 1) * LANES)
        qb_ref[rows, sl] = (group(_C_QB, j) * (scale * LOG2_E)).astype(BF16)
        v = group(_C_VB, j)
        vblo_ref[rows, sl] = jnp.where(lower_head, v, 0.0).astype(BF16)
        vbhi_ref[rows, sl] = jnp.where(lower_head, 0.0, v).astype(BF16)

    kt_all = _dot_nt(wkt_ref[...], u)

    cos_f = cosf_ref[:, rows]
    sin_f = sinf_ref[:, rows]
    half = HEAD_DIM // 2
    zeros = jnp.zeros((HEAD_DIM, tm), F32)
    lo_parts, hi_parts = [], []
    for c in range(N_KV_A):
        x1 = kt_all[c * HEAD_DIM:c * HEAD_DIM + half]
        x2 = kt_all[c * HEAD_DIM + half:(c + 1) * HEAD_DIM]
        roped = jnp.concatenate([x1 * cos_f - x2 * sin_f, x2 * cos_f + x1 * sin_f], axis=0)
        lo_parts += [roped, zeros]
        hi_parts += [zeros, roped]
    kat_lo = jnp.concatenate(lo_parts, axis=0).astype(BF16)
    kat_hi = jnp.concatenate(hi_parts, axis=0).astype(BF16)

    kt = kt_all[KV_WIDTH_A:]
    lower_feat = (lax.broadcasted_iota(jnp.int32, kt.shape, 0) & HEAD_DIM) == 0
    kt_lo = jnp.where(lower_feat, kt, 0.0).astype(BF16)
    kt_hi = jnp.where(lower_feat, 0.0, kt).astype(BF16)
    for j in range(tm // BLOCK):
        cols = slice(j * BLOCK, (j + 1) * BLOCK)
        katlo_ref[kb0 + j] = kat_lo[:, cols]
        kathi_ref[kb0 + j] = kat_hi[:, cols]
        ktlo_ref[kb0 + j] = kt_lo[:, cols]
        kthi_ref[kb0 + j] = kt_hi[:, cols]


def _in_proj(x2d, gain, w_cat, w_kt, tables, row_tile):
    n, d = x2d.shape
    cos_t, sin_t, cos_f, sin_f = tables
    table_tiles = cos_t.shape[0] // row_tile
    kb_per_tile = row_tile // BLOCK
    row_spec = lambda w: pl.BlockSpec((row_tile, w), lambda i: (i, 0))
    const_spec = lambda shape: pl.BlockSpec(shape, lambda i: (0, 0), pipeline_mode=pl.Buffered(1))
    table_spec = pl.BlockSpec((row_tile, LANES), lambda i: (i % table_tiles, 0))
    ftable_spec = pl.BlockSpec((HEAD_DIM // 2, row_tile), lambda i: (0, i % table_tiles))
    row_out = lambda w: (row_spec(w), jax.ShapeDtypeStruct((n, w), BF16))
    kt_out = lambda w: (pl.BlockSpec((kb_per_tile, w, BLOCK), lambda i: (i, 0, 0)),
                        jax.ShapeDtypeStruct((n // BLOCK, w, BLOCK), BF16))
    outs = [row_out(WIDTH_A), kt_out(KV_DUP_A), kt_out(KV_DUP_A), row_out(KV_DUP_A), row_out(KV_DUP_A),
            row_out(WIDTH_B), kt_out(WIDTH_B), kt_out(WIDTH_B), row_out(WIDTH_B), row_out(WIDTH_B)]
    return pl.pallas_call(
        functools.partial(_in_proj_kernel, min(row_tile, PROJ_SUB_ROWS)),
        grid=(n // row_tile,),
        in_specs=[row_spec(d), const_spec((1, d)), const_spec(w_cat.shape), const_spec(w_kt.shape),
                  table_spec, table_spec, ftable_spec, ftable_spec],
        out_specs=[o[0] for o in outs],
        out_shape=[o[1] for o in outs],
        compiler_params=pltpu.CompilerParams(
            dimension_semantics=("arbitrary",), vmem_limit_bytes=VMEM_LIMIT),
        name="in_proj",
    )(x2d, gain, w_cat, w_kt, cos_t, sin_t, cos_f, sin_f)


def _attention_kernel(*refs):
    qi = pl.program_id(1)

    @pl.when(qi == 0)
    def _():
        _attention_tile(0, *refs)

    @pl.when(qi > 0)
    def _():
        _attention_tile(qi, *refs)


def _attention_tile(qi, sinks_ref, qa_ref, katlo_ref, kathi_ref, valo_ref, vahi_ref,
                    qb_ref, ktlo_ref, kthi_ref, vblo_ref, vbhi_ref,
                    mwa_ref, mva_ref, onesa_ref, mktlo_ref, mkthi_ref, mvblo_ref, mvbhi_ref,
                    ga_ref, gb_ref, tri_ref, out_ref, carry_ref, oacc_ref, zbuf_ref):
    first_tile = isinstance(qi, int)
    blocks_per_tile = Q_TILE // BLOCK
    assert blocks_per_tile == 2, "the group-B sweep peels exactly two diagonal key blocks"

    def row_start(blk):
        return blk * BLOCK if isinstance(blk, int) else pl.multiple_of(blk * BLOCK, BLOCK)

    in_cur = ((lax.broadcasted_iota(jnp.int32, (BLOCK, 2 * BLOCK), 1) & (BLOCK - 1))
              <= lax.broadcasted_iota(jnp.int32, (BLOCK, 2 * BLOCK), 0))
    in_cur1 = in_cur[:, :BLOCK]
    lane = lax.broadcasted_iota(jnp.int32, (BLOCK, LANES), 1)
    meta_lo = lane < N_META
    meta_any = lane < 2 * N_META
    lower_lanes = lane < HEAD_DIM
    neg_inf = -jnp.inf
    chains = [(half, g) for half in range(blocks_per_tile) for g in range(N_PAIRS_A)]

    def a_blocks(half):
        blk = qi * blocks_per_tile + half
        return blk, (None if first_tile and half == 0 else blk - 1)

    def a_keys(blk, csl):
        return jnp.concatenate([katlo_ref[0, blk, csl, :], kathi_ref[0, blk, csl, :]], axis=1)

    scores = []
    for half, g in chains:
        blk, prev = a_blocks(half)
        c = (2 * g) // (N_HEADS_A // N_KV_A)
        csl = slice(c * LANES, (c + 1) * LANES)
        q2 = qa_ref[0, half * BLOCK:(half + 1) * BLOCK, g * LANES:(g + 1) * LANES]
        s_prev = neg_inf if prev is None else _dot(q2, a_keys(prev, csl))
        scores.append((_dot(q2, a_keys(blk, csl)), s_prev, _dot(q2, mwa_ref[c])))
    probs = []
    for (half, g), (s_cur, s_prev, sm) in zip(chains, scores):
        s = jnp.where(in_cur, s_cur, s_prev)
        s_lo, s_hi = s[:, :BLOCK], s[:, BLOCK:]
        sink_lo, sink_hi = sinks_ref[2 * g] * LOG2_E, sinks_ref[2 * g + 1] * LOG2_E
        m_lo = jnp.maximum(jnp.max(jnp.maximum(s_lo, jnp.where(meta_lo, sm, neg_inf)),
                                   axis=-1, keepdims=True), sink_lo)
        m_hi = jnp.maximum(jnp.max(jnp.maximum(s_hi, jnp.where(meta_lo | ~meta_any, neg_inf, sm)),
                                   axis=-1, keepdims=True), sink_hi)
        p_lo = jnp.exp2(s_lo - m_lo)
        p_hi = jnp.exp2(s_hi - m_hi)
        pm = jnp.where(meta_any, jnp.exp2(sm - jnp.where(meta_lo, m_lo, m_hi)), 0.0)
        pcat = jnp.concatenate([jnp.where(in_cur1, 0.0, p_lo), jnp.where(in_cur1, 0.0, p_hi),
                                jnp.where(in_cur1, p_lo, 0.0), jnp.where(in_cur1, p_hi, 0.0)],
                               axis=1).astype(BF16)
        sink_term = jnp.where(lower_lanes, jnp.exp2(sink_lo - m_lo), jnp.exp2(sink_hi - m_hi))
        probs.append((pcat, pm.astype(BF16), sink_term))
    ones_a = onesa_ref[...]
    for half in range(blocks_per_tile):
        blk, prev = a_blocks(half)
        cur0 = row_start(blk)
        prev0 = cur0 if prev is None else row_start(prev)
        oa = []
        for g in range(N_PAIRS_A):
            c = (2 * g) // (N_HEADS_A // N_KV_A)
            csl = slice(c * LANES, (c + 1) * LANES)
            pcat, pm, sink_term = probs[half * N_PAIRS_A + g]
            vcat = jnp.concatenate(
                [valo_ref[0, pl.ds(prev0, BLOCK), csl], vahi_ref[0, pl.ds(prev0, BLOCK), csl],
                 valo_ref[0, pl.ds(cur0, BLOCK), csl], vahi_ref[0, pl.ds(cur0, BLOCK), csl]], axis=0)
            res = _dot(pcat, jnp.concatenate([vcat, ones_a], axis=1)) + _dot(pm, mva_ref[c])
            oa.append(res[:, :LANES] / (res[:, LANES:] + sink_term))
        oa = jnp.concatenate(oa, axis=1)
        out_ref[0, half * BLOCK:(half + 1) * BLOCK, :WIDTH_A] = _rms(oa, ga_ref[...]).astype(out_ref.dtype)

    carry_ref[...] = jnp.zeros_like(carry_ref)
    oacc_ref[...] = jnp.zeros_like(oacc_ref)
    tri = tri_ref[...]
    k_minus_q = (lax.broadcasted_iota(jnp.int32, (Q_TILE, LANES), 1)
                 - lax.broadcasted_iota(jnp.int32, (Q_TILE, LANES), 0))
    meta_valid = lax.broadcasted_iota(jnp.int32, (Q_TILE, LANES), 1) >= PAD

    def pair_lanes(g):
        return slice(g * LANES, (g + 1) * LANES)

    def pair_scores(g, rows, load_kt):
        return _dot(qb_ref[0, rows, pair_lanes(g)], jnp.concatenate(load_kt(pair_lanes(g)), axis=1))

    def scores(rows, load_kt):
        return [pair_scores(g, rows, load_kt) for g in range(N_PAIRS_B)]

    def masked(t, valid):
        n = valid.shape[0]
        head = jnp.where(valid, t[:n], 0.0)
        return head if n == t.shape[0] else jnp.concatenate([head, t[n:]], axis=0)

    def b_step(zs, load_v, valid, rows=slice(0, Q_TILE), carry_max_rows=None, ahead=None):
        rs, row_sums = [], []
        valid2 = None if valid is None else jnp.concatenate([valid, valid], axis=1)
        for g in range(N_PAIRS_B):
            z = zs[g]
            sp = jnp.maximum(z, 0.0) + jnp.log2(1.0 + jnp.exp2(-jnp.abs(z)))
            if valid2 is not None:
                sp = masked(sp, valid2)
            rs.append(_dot(sp.astype(BF16), tri))
            row_sums.append([jnp.sum(sp[:, h * BLOCK:(h + 1) * BLOCK], axis=-1, keepdims=True)
                             for h in range(2)])
            if ahead is not None:
                slot, load_kt, ahead_rows = ahead
                zbuf_ref[slot, g, ahead_rows] = pair_scores(g, ahead_rows, load_kt)
        carry_max = None
        for g in range(N_PAIRS_B):
            a_heads = []
            for h in range(2):
                keys = slice(h * BLOCK, (h + 1) * BLOCK)
                carry = carry_ref[g, h, rows]
                a = jnp.exp2(zs[g][:, keys] + rs[g][:, keys] + carry)
                if valid is not None:
                    a = masked(a, valid)
                carry = carry - row_sums[g][h]
                carry_ref[g, h, rows] = carry
                if carry_max_rows is not None:
                    watched = carry[carry_max_rows]
                    carry_max = watched if carry_max is None else jnp.maximum(carry_max, watched)
                a_heads.append(a.astype(BF16))
            oacc_ref[rows, pair_lanes(g)] += _dot(jnp.concatenate(a_heads, axis=1),
                                                  jnp.concatenate(load_v(pair_lanes(g)), axis=0))
        return None if carry_max is None else jnp.max(carry_max)

    def real_kt(kb):
        return lambda sl: (ktlo_ref[0, kb, sl, :], kthi_ref[0, kb, sl, :])

    def real_v(kb):
        r0 = row_start(kb)
        return lambda sl: (vblo_ref[0, pl.ds(r0, BLOCK), sl], vbhi_ref[0, pl.ds(r0, BLOCK), sl])

    def stashed(slot, rows):
        return [zbuf_ref[slot, g, rows] for g in range(N_PAIRS_B)]

    all_rows = slice(0, Q_TILE)
    upper_rows = slice(BLOCK, Q_TILE)
    n_full = qi * blocks_per_tile
    top = n_full + 1
    meta_kt = lambda sl: (mktlo_ref[0, sl, :], mkthi_ref[0, sl, :])
    meta_v = lambda sl: (mvblo_ref[:, sl], mvbhi_ref[:, sl])
    lower_rows = slice(0, BLOCK)
    b_step(scores(upper_rows, real_kt(top)), real_v(top), k_minus_q[upper_rows] < -BLOCK,
           rows=upper_rows, ahead=(0, real_kt(top - 1), all_rows))
    b_step(stashed(0, all_rows), real_v(top - 1), k_minus_q[lower_rows] < 0,
           ahead=(1, meta_kt if first_tile else real_kt(n_full - 1), all_rows))

    if first_tile:
        b_step(stashed(1, all_rows), meta_v, meta_valid)
    else:
        upper_max = b_step(stashed(1, all_rows), real_v(n_full - 1), None, carry_max_rows=upper_rows,
                           ahead=(0, real_kt(n_full - 2), lower_rows))
        every_row = slice(None)
        lower_max = b_step(stashed(0, lower_rows), real_v(n_full - 2), None, rows=lower_rows,
                           carry_max_rows=every_row)

        def alive(carry_max):
            return (carry_max >= DEAD_LOG2).astype(jnp.int32)

        def tail_cond(state):
            j, live = state
            return jnp.logical_and(j < n_full - 1, live > 0)

        def tail_body(state):
            j, _ = state
            kb_upper = n_full - 2 - j
            kb_lower = jnp.maximum(kb_upper - 1, 0)
            lower_exists = k_minus_q[lower_rows] < jnp.where(kb_upper >= 1, Q_TILE, -Q_TILE)
            upper_max = b_step(scores(upper_rows, real_kt(kb_upper)), real_v(kb_upper), None,
                               rows=upper_rows, carry_max_rows=every_row)
            lower_max = b_step(scores(lower_rows, real_kt(kb_lower)), real_v(kb_lower), lower_exists,
                               rows=lower_rows, carry_max_rows=every_row)
            return j + 1, alive(jnp.maximum(upper_max, lower_max))

        _, live = lax.while_loop(tail_cond, tail_body,
                                 (jnp.int32(0), alive(jnp.maximum(upper_max, lower_max))))

        @pl.when(live > 0)
        def _():
            b_step(scores(all_rows, meta_kt), meta_v, meta_valid)

    out_ref[0, :, WIDTH_A:] = _rms(oacc_ref[...], gb_ref[...]).astype(out_ref.dtype)


def _attention(sinks, real, meta, gain_a, gain_b, tri, batch, seq):
    nq = seq // Q_TILE
    nkb = seq // BLOCK
    qa, katlo, kathi, valo, vahi, qb, ktlo, kthi, vblo, vbhi = real
    rows3 = lambda t: t.reshape(batch, seq, t.shape[-1])
    qa, valo, vahi, qb, vblo, vbhi = [rows3(t) for t in (qa, valo, vahi, qb, vblo, vbhi)]
    slabs = lambda t: t.reshape(batch, nkb, t.shape[-2], BLOCK)
    katlo, kathi, ktlo, kthi = [slabs(t) for t in (katlo, kathi, ktlo, kthi)]
    _, mkatlo, mkathi, mvalo, mvahi, _, mktlo, mkthi, mvblo, mvbhi = meta
    pairs = range(KV_DUP_A // LANES)
    pair = lambda c: slice(c * LANES, (c + 1) * LANES)
    mwa = jnp.stack([jnp.pad(jnp.concatenate([mkatlo[0, pair(c), PAD:], mkathi[0, pair(c), PAD:]], axis=1),
                             ((0, 0), (0, LANES - 2 * N_META))) for c in pairs])
    mva = jnp.stack([jnp.pad(jnp.concatenate([mvalo[PAD:, pair(c)], mvahi[PAD:, pair(c)]], axis=0),
                             ((0, LANES - 2 * N_META), (0, 0))) for c in pairs])
    lower_lane = jnp.arange(LANES)[None, :] < HEAD_DIM
    head_ones = lambda upper_row: (lower_lane != upper_row[:, None]).astype(BF16)
    ones_a = head_ones((jnp.arange(4 * BLOCK) // BLOCK) % 2 == 1)
    meta_row = jnp.arange(LANES)
    meta_ones = jnp.where((meta_row < 2 * N_META)[:, None], head_ones(meta_row >= N_META), 0).astype(BF16)
    mva = jnp.concatenate([mva, jnp.broadcast_to(meta_ones, mva.shape)], axis=2)
    qspec = lambda w: pl.BlockSpec((1, Q_TILE, w), lambda b, i: (b, i, 0))
    seqspec = lambda w: pl.BlockSpec((1, seq, w), lambda b, i: (b, 0, 0))
    ktspec = lambda w: pl.BlockSpec((1, nkb, w, BLOCK), lambda b, i: (b, 0, 0, 0))
    const = lambda a: pl.BlockSpec(a.shape, lambda b, i: (0,) * a.ndim)
    mix = WIDTH_A + WIDTH_B
    return pl.pallas_call(
        _attention_kernel,
        grid=(batch, nq),
        in_specs=[pl.BlockSpec(memory_space=pltpu.SMEM),
                  qspec(WIDTH_A), ktspec(KV_DUP_A), ktspec(KV_DUP_A), seqspec(KV_DUP_A), seqspec(KV_DUP_A),
                  qspec(WIDTH_B), ktspec(WIDTH_B), ktspec(WIDTH_B), seqspec(WIDTH_B), seqspec(WIDTH_B),
                  const(mwa), const(mva), const(ones_a), const(mktlo), const(mkthi),
                  const(mvblo), const(mvbhi), const(gain_a), const(gain_b), const(tri)],
        out_specs=pl.BlockSpec((1, Q_TILE, mix), lambda b, i: (b, i, 0)),
        out_shape=jax.ShapeDtypeStruct((batch, seq, mix), BF16),
        scratch_shapes=[pltpu.VMEM((N_PAIRS_B, 2, Q_TILE, LANES), F32),
                        pltpu.VMEM((Q_TILE, WIDTH_B), F32),
                        pltpu.VMEM((2, N_PAIRS_B, Q_TILE, 2 * BLOCK), F32)],
        compiler_params=pltpu.CompilerParams(
            dimension_semantics=("arbitrary", "arbitrary"), vmem_limit_bytes=VMEM_LIMIT),
        name="attention",
    )(sinks, qa, katlo, kathi, valo, vahi, qb, ktlo, kthi, vblo, vbhi, mwa, mva, ones_a, mktlo, mkthi,
      mvblo, mvbhi, gain_a, gain_b, tri)


def _out_ffn_kernel(ff_chunks, n_sub, mixed_ref, x_ref, wo_ref, gf_ref, wg_ref, wu_ref, wd_ref, gl_ref,
                    out_ref):
    sub = x_ref.shape[0] // n_sub
    rows = [slice(i * sub, (i + 1) * sub) for i in range(n_sub)]
    hs = [x_ref[r, :] + _dot(mixed_ref[r, :], wo_ref[...]) for r in rows]
    for r, h in zip(rows, hs):
        u = _rms(h, gf_ref[...]).astype(BF16)
        y = h
        for c0, c1 in ff_chunks:
            gate = _dot(u, wg_ref[:, c0:c1])
            up = _dot(u, wu_ref[:, c0:c1])
            act = (gate * jax.nn.sigmoid(gate) * up).astype(BF16)
            y = y + _dot(act, wd_ref[c0:c1, :])
        out_ref[r, :] = _rms(y, gl_ref[...])


def _ff_chunks(d_ff, n_chunks, col_tile):
    tiles = -(-d_ff // col_tile)
    bounds = [min(d_ff, (tiles * k // n_chunks) * col_tile) for k in range(n_chunks + 1)]
    return tuple((bounds[k], bounds[k + 1]) for k in range(n_chunks))


def _out_ffn(mixed2d, x2d, w_out, g_ffn, w_gate, w_up, w_down, g_final, row_tile):
    n, d = x2d.shape
    d_ff = w_gate.shape[1]
    row_spec = lambda w: pl.BlockSpec((row_tile, w), lambda i: (i, 0))
    const = lambda a: pl.BlockSpec(a.shape, lambda i: (0, 0), pipeline_mode=pl.Buffered(1))
    mxu_cols = 256
    return pl.pallas_call(
        functools.partial(_out_ffn_kernel, _ff_chunks(d_ff, 2, mxu_cols), FFN_SUB_TILES),
        grid=(n // row_tile,),
        in_specs=[row_spec(mixed2d.shape[1]), row_spec(d), const(w_out), const(g_ffn),
                  const(w_gate), const(w_up), const(w_down), const(g_final)],
        out_specs=row_spec(d),
        out_shape=jax.ShapeDtypeStruct((n, d), x2d.dtype),
        compiler_params=pltpu.CompilerParams(
            dimension_semantics=("arbitrary",), vmem_limit_bytes=VMEM_LIMIT),
        name="out_ffn",
    )(mixed2d, x2d, w_out, g_ffn, w_gate, w_up, w_down, g_final)


def _rope_tables(pos):
    half = HEAD_DIM // 2
    inv_freq = ROPE_THETA ** (-jnp.arange(half, dtype=F32) / half)
    ang = pos.astype(F32)[:, None] * inv_freq[None, :]
    cos, sin = jnp.cos(ang), jnp.sin(ang)
    reps = LANES // HEAD_DIM
    return (jnp.tile(jnp.concatenate([cos, cos], axis=1), (1, reps)),
            jnp.tile(jnp.concatenate([-sin, sin], axis=1), (1, reps)), cos.T, sin.T)


def _rearranged_w_in(w):
    o_ka = WIDTH_A
    o_va = o_ka + KV_WIDTH_A
    o_qb = o_va + KV_WIDTH_A
    o_kb = o_qb + WIDTH_B
    o_vb = o_kb + WIDTH_B
    dup = lambda t: jnp.repeat(t.reshape(t.shape[0], N_KV_A, 1, HEAD_DIM), 2, axis=2).reshape(t.shape[0], KV_DUP_A)
    w_cat = jnp.concatenate([w[:, :o_ka], dup(w[:, o_va:o_qb]), w[:, o_qb:o_kb], w[:, o_vb:]],
                            axis=1).astype(BF16)
    w_kt = jnp.concatenate([w[:, o_ka:o_va], w[:, o_kb:o_vb]], axis=1).T.astype(BF16)
    return w_cat, w_kt


def kernel(x, meta_tokens, norm_mix, w_in, sinks, norm_out_a, norm_out_b, w_out,
           norm_ffn, w_gate, w_up, w_down, norm_final):
    batch, seq, d = x.shape
    assert norm_mix.shape[0] == 1, "single-layer block: meta rows are not carried past the mixer"
    assert seq % ROW_TILE_PROJ == 0 and (batch * seq) % ROW_TILE_FFN == 0 and seq % Q_TILE == 0
    x2d = x.reshape(batch * seq, d)
    w_cat, w_kt = _rearranged_w_in(w_in[0])
    assert w_cat.shape[1] == _C_END
    g_mix = norm_mix[0].reshape(1, d)

    meta_block = jnp.concatenate([jnp.zeros((PAD, d), x.dtype), meta_tokens.astype(x.dtype)], axis=0)
    real = _in_proj(x2d, g_mix, w_cat, w_kt, _rope_tables(jnp.arange(seq) + N_META), ROW_TILE_PROJ)
    meta = _in_proj(meta_block, g_mix, w_cat, w_kt, _rope_tables(jnp.arange(BLOCK) - PAD), BLOCK)

    j = jnp.arange(2 * BLOCK)[:, None]
    s = jnp.arange(2 * BLOCK)[None, :]
    tri = jnp.where((j // BLOCK == s // BLOCK) & (j >= s), -1.0, 0.0).astype(BF16)

    mixed = _attention(sinks[0], real, meta, norm_out_a[0].reshape(1, WIDTH_A),
                       norm_out_b[0].reshape(1, WIDTH_B), tri, batch, seq)

    out = _out_ffn(mixed.reshape(batch * seq, WIDTH_A + WIDTH_B), x2d, w_out[0].astype(BF16),
                   norm_ffn[0].reshape(1, d), w_gate[0].astype(BF16), w_up[0].astype(BF16),
                   w_down[0].astype(BF16), norm_final.reshape(1, d), ROW_TILE_FFN)
    return out.reshape(batch, seq, d)
```

```python
import functools

import jax
import jax.numpy as jnp
from jax import lax
from jax.experimental import pallas as pl
from jax.experimental.pallas import tpu as pltpu

HEAD_DIM = 64
N_HEADS_A = 8
N_KV_A = 2
N_HEADS_B = 8
WIDTH_A = N_HEADS_A * HEAD_DIM
KV_WIDTH_A = N_KV_A * HEAD_DIM
WIDTH_B = N_HEADS_B * HEAD_DIM
BLOCK = 128
N_META = 16
PAD = BLOCK - N_META
ROPE_THETA = 10000.0
EPS = 1e-6
LANES = 128
V7X_MXU_COLS = 256
V7X_VMEM_BYTES = 64 * 1024 * 1024
N_PAIRS_A = WIDTH_A // LANES
N_PAIRS_B = WIDTH_B // LANES
KV_DUP_A = 2 * KV_WIDTH_A

Q_TILE = 2 * BLOCK
YOUNG_ROWS = 32
ROW_TILE_PROJ = 1024
PROJ_SUB_ROWS = 256
ROW_TILE_FFN = 1024
FFN_SUB_TILES = 4
VMEM_LIMIT = V7X_VMEM_BYTES * 7 // 8

F32 = jnp.float32
BF16 = jnp.bfloat16
LOG2_E = 1.4426950408889634
DEAD_LOG2 = -150.0

_C_QA = 0
_C_VA = _C_QA + WIDTH_A
_C_QB = _C_VA + KV_DUP_A
_C_VB = _C_QB + WIDTH_B
_C_END = _C_VB + WIDTH_B


def _rms(xf, g):
    ms = jnp.mean(xf * xf, axis=-1, keepdims=True)
    return xf * lax.rsqrt(ms + EPS) * g


def _dot(a, b):
    return jnp.dot(a, b, preferred_element_type=F32)


def _dot_nt(a, b):
    return lax.dot_general(a, b, (((1,), (1,)), ((), ())), preferred_element_type=F32)


def _in_proj_kernel(sub_rows, *refs):
    for r0 in range(0, refs[0].shape[0], sub_rows):
        _in_proj_rows(slice(r0, r0 + sub_rows), r0 // BLOCK, *refs)


def _in_proj_rows(rows, kb0, x_ref, g_ref, w_ref, wkt_ref, cos_ref, sin_ref, cosf_ref, sinf_ref,
                  qa_ref, katlo_ref, kathi_ref, valo_ref, vahi_ref,
                  qb_ref, ktlo_ref, kthi_ref, vblo_ref, vbhi_ref):
    tm = rows.stop - rows.start
    u = _rms(x_ref[rows, :], g_ref[...]).astype(BF16)
    proj = _dot(u, w_ref[...])
    cos = cos_ref[rows, :]
    sin = sin_ref[rows, :]
    lane = lax.broadcasted_iota(jnp.int32, (tm, LANES), 1)
    first_half = (lane & (HEAD_DIM // 2)) == 0
    lower_head = lane < HEAD_DIM
    scale = HEAD_DIM ** -0.5

    def rope(t):
        rot = jnp.where(first_half,
                        pltpu.roll(t, LANES - HEAD_DIM // 2, 1),
                        pltpu.roll(t, HEAD_DIM // 2, 1))
        return t * cos + rot * sin

    def group(c0, j):
        return proj[:, c0 + j * LANES:c0 + (j + 1) * LANES]

    for j in range(N_PAIRS_A):
        sl = slice(j * LANES, (j + 1) * LANES)
        qa_ref[rows, sl] = (rope(group(_C_QA, j)) * (scale * LOG2_E)).astype(BF16)
    for j in range(KV_DUP_A // LANES):
        sl = slice(j * LANES, (j + 1) * LANES)
        v = group(_C_VA, j)
        valo_ref[rows, sl] = jnp.where(lower_head, v, 0.0).astype(BF16)
        vahi_ref[rows, sl] = jnp.where(lower_head, 0.0, v).astype(BF16)
    for j in range(N_PAIRS_B):
        sl = slice(j * LANES, (j + 1) * LANES)
        qb_ref[rows, sl] = (group(_C_QB, j) * (scale * LOG2_E)).astype(BF16)
        v = group(_C_VB, j)
        vblo_ref[rows, sl] = jnp.where(lower_head, v, 0.0).astype(BF16)
        vbhi_ref[rows, sl] = jnp.where(lower_head, 0.0, v).astype(BF16)

    kt_all = _dot_nt(wkt_ref[...], u)

    cos_f = cosf_ref[:, rows]
    sin_f = sinf_ref[:, rows]
    half = HEAD_DIM // 2
    zeros = jnp.zeros((HEAD_DIM, tm), F32)
    lo_parts, hi_parts = [], []
    for c in range(N_KV_A):
        x1 = kt_all[c * HEAD_DIM:c * HEAD_DIM + half]
        x2 = kt_all[c * HEAD_DIM + half:(c + 1) * HEAD_DIM]
        roped = jnp.concatenate([x1 * cos_f - x2 * sin_f, x2 * cos_f + x1 * sin_f], axis=0)
        lo_parts += [roped, zeros]
        hi_parts += [zeros, roped]
    kat_lo = jnp.concatenate(lo_parts, axis=0).astype(BF16)
    kat_hi = jnp.concatenate(hi_parts, axis=0).astype(BF16)

    kt = kt_all[KV_WIDTH_A:]
    lower_feat = (lax.broadcasted_iota(jnp.int32, kt.shape, 0) & HEAD_DIM) == 0
    kt_lo = jnp.where(lower_feat, kt, 0.0).astype(BF16)
    kt_hi = jnp.where(lower_feat, 0.0, kt).astype(BF16)
    for j in range(tm // BLOCK):
        cols = slice(j * BLOCK, (j + 1) * BLOCK)
        katlo_ref[kb0 + j] = kat_lo[:, cols]
        kathi_ref[kb0 + j] = kat_hi[:, cols]
        ktlo_ref[kb0 + j] = kt_lo[:, cols]
        kthi_ref[kb0 + j] = kt_hi[:, cols]


def _in_proj(x2d, gain, w_cat, w_kt, tables, row_tile):
    n, d = x2d.shape
    cos_t, sin_t, cos_f, sin_f = tables
    table_tiles = cos_t.shape[0] // row_tile
    kb_per_tile = row_tile // BLOCK
    row_spec = lambda w: pl.BlockSpec((row_tile, w), lambda i: (i, 0))
    const_spec = lambda shape: pl.BlockSpec(shape, lambda i: (0, 0), pipeline_mode=pl.Buffered(1))
    table_spec = pl.BlockSpec((row_tile, LANES), lambda i: (i % table_tiles, 0))
    ftable_spec = pl.BlockSpec((HEAD_DIM // 2, row_tile), lambda i: (0, i % table_tiles))
    row_out = lambda w: (row_spec(w), jax.ShapeDtypeStruct((n, w), BF16))
    kt_out = lambda w: (pl.BlockSpec((kb_per_tile, w, BLOCK), lambda i: (i, 0, 0)),
                        jax.ShapeDtypeStruct((n // BLOCK, w, BLOCK), BF16))
    outs = [row_out(WIDTH_A), kt_out(KV_DUP_A), kt_out(KV_DUP_A), row_out(KV_DUP_A), row_out(KV_DUP_A),
            row_out(WIDTH_B), kt_out(WIDTH_B), kt_out(WIDTH_B), row_out(WIDTH_B), row_out(WIDTH_B)]
    return pl.pallas_call(
        functools.partial(_in_proj_kernel, min(row_tile, PROJ_SUB_ROWS)),
        grid=(n // row_tile,),
        in_specs=[row_spec(d), const_spec((1, d)), const_spec(w_cat.shape), const_spec(w_kt.shape),
                  table_spec, table_spec, ftable_spec, ftable_spec],
        out_specs=[o[0] for o in outs],
        out_shape=[o[1] for o in outs],
        compiler_params=pltpu.CompilerParams(
            dimension_semantics=("arbitrary",), vmem_limit_bytes=VMEM_LIMIT),
        name="in_proj",
    )(x2d, gain, w_cat, w_kt, cos_t, sin_t, cos_f, sin_f)


def _attention_kernel(*refs):
    qi = pl.program_id(1)

    @pl.when(qi == 0)
    def _():
        _attention_tile(0, *refs)

    @pl.when(qi > 0)
    def _():
        _attention_tile(qi, *refs)


def _attention_tile(qi, sinks_ref, qa_ref, katlo_ref, kathi_ref, valo_ref, vahi_ref,
                    qb_ref, ktlo_ref, kthi_ref, vblo_ref, vbhi_ref,
                    mwa_ref, mva_ref, onesa_ref, mktlo_ref, mkthi_ref, mvblo_ref, mvbhi_ref,
                    ga_ref, gb_ref, tri_ref, out_ref, carry_ref, oacc_ref, zbuf_ref):
    first_tile = isinstance(qi, int)
    blocks_per_tile = Q_TILE // BLOCK
    assert blocks_per_tile == 2, "the group-B sweep peels exactly two diagonal key blocks"

    def row_start(blk):
        return blk * BLOCK if isinstance(blk, int) else pl.multiple_of(blk * BLOCK, BLOCK)

    in_cur = ((lax.broadcasted_iota(jnp.int32, (BLOCK, 2 * BLOCK), 1) & (BLOCK - 1))
              <= lax.broadcasted_iota(jnp.int32, (BLOCK, 2 * BLOCK), 0))
    in_cur1 = in_cur[:, :BLOCK]
    lane = lax.broadcasted_iota(jnp.int32, (BLOCK, LANES), 1)
    meta_lo = lane < N_META
    meta_any = lane < 2 * N_META
    lower_lanes = lane < HEAD_DIM
    neg_inf = -jnp.inf
    chains = [(half, g) for half in range(blocks_per_tile) for g in range(N_PAIRS_A)]

    def a_blocks(half):
        blk = qi * blocks_per_tile + half
        return blk, (None if first_tile and half == 0 else blk - 1)

    def a_keys(blk, csl):
        return jnp.concatenate([katlo_ref[0, blk, csl, :], kathi_ref[0, blk, csl, :]], axis=1)

    scores = []
    for half, g in chains:
        blk, prev = a_blocks(half)
        c = (2 * g) // (N_HEADS_A // N_KV_A)
        csl = slice(c * LANES, (c + 1) * LANES)
        q2 = qa_ref[0, half * BLOCK:(half + 1) * BLOCK, g * LANES:(g + 1) * LANES]
        s_prev = neg_inf if prev is None else _dot(q2, a_keys(prev, csl))
        scores.append((_dot(q2, a_keys(blk, csl)), s_prev, _dot(q2, mwa_ref[c])))
    probs = []
    for (half, g), (s_cur, s_prev, sm) in zip(chains, scores):
        s = jnp.where(in_cur, s_cur, s_prev)
        s_lo, s_hi = s[:, :BLOCK], s[:, BLOCK:]
        sink_lo, sink_hi = sinks_ref[2 * g] * LOG2_E, sinks_ref[2 * g + 1] * LOG2_E
        m_lo = jnp.maximum(jnp.max(jnp.maximum(s_lo, jnp.where(meta_lo, sm, neg_inf)),
                                   axis=-1, keepdims=True), sink_lo)
        m_hi = jnp.maximum(jnp.max(jnp.maximum(s_hi, jnp.where(meta_lo | ~meta_any, neg_inf, sm)),
                                   axis=-1, keepdims=True), sink_hi)
        p_lo = jnp.exp2(s_lo - m_lo)
        p_hi = jnp.exp2(s_hi - m_hi)
        pm = jnp.where(meta_any, jnp.exp2(sm - jnp.where(meta_lo, m_lo, m_hi)), 0.0)
        pcat = jnp.concatenate([jnp.where(in_cur1, 0.0, p_lo), jnp.where(in_cur1, 0.0, p_hi),
                                jnp.where(in_cur1, p_lo, 0.0), jnp.where(in_cur1, p_hi, 0.0)],
                               axis=1).astype(BF16)
        sink_term = jnp.where(lower_lanes, jnp.exp2(sink_lo - m_lo), jnp.exp2(sink_hi - m_hi))
        probs.append((pcat, pm.astype(BF16), sink_term))
    ones_a = onesa_ref[...]
    for half in range(blocks_per_tile):
        blk, prev = a_blocks(half)
        cur0 = row_start(blk)
        prev0 = cur0 if prev is None else row_start(prev)
        oa = []
        for g in range(N_PAIRS_A):
            c = (2 * g) // (N_HEADS_A // N_KV_A)
            csl = slice(c * LANES, (c + 1) * LANES)
            pcat, pm, sink_term = probs[half * N_PAIRS_A + g]
            vcat = jnp.concatenate(
                [valo_ref[0, pl.ds(prev0, BLOCK), csl], vahi_ref[0, pl.ds(prev0, BLOCK), csl],
                 valo_ref[0, pl.ds(cur0, BLOCK), csl], vahi_ref[0, pl.ds(cur0, BLOCK), csl]], axis=0)
            res = _dot(pcat, jnp.concatenate([vcat, ones_a], axis=1)) + _dot(pm, mva_ref[c])
            oa.append(res[:, :LANES] / (res[:, LANES:] + sink_term))
        oa = jnp.concatenate(oa, axis=1)
        out_ref[0, half * BLOCK:(half + 1) * BLOCK, :WIDTH_A] = _rms(oa, ga_ref[...]).astype(out_ref.dtype)

    carry_ref[...] = jnp.zeros_like(carry_ref)
    oacc_ref[...] = jnp.zeros_like(oacc_ref)
    tri = tri_ref[...]
    k_minus_q = (lax.broadcasted_iota(jnp.int32, (Q_TILE, LANES), 1)
                 - lax.broadcasted_iota(jnp.int32, (Q_TILE, LANES), 0))
    meta_valid = lax.broadcasted_iota(jnp.int32, (Q_TILE, LANES), 1) >= PAD

    def pair_lanes(g):
        return slice(g * LANES, (g + 1) * LANES)

    def pair_scores(g, rows, load_kt):
        return _dot(qb_ref[0, rows, pair_lanes(g)], jnp.concatenate(load_kt(pair_lanes(g)), axis=1))

    def scores(rows, load_kt):
        return [pair_scores(g, rows, load_kt) for g in range(N_PAIRS_B)]

    def masked(t, valid):
        n = valid.shape[0]
        head = jnp.where(valid, t[:n], 0.0)
        return head if n == t.shape[0] else jnp.concatenate([head, t[n:]], axis=0)

    def b_step(zs, load_v, valid, rows=slice(0, Q_TILE), carry_max_rows=None, ahead=None):
        rs, row_sums = [], []
        valid2 = None if valid is None else jnp.concatenate([valid, valid], axis=1)
        for g in range(N_PAIRS_B):
            z = zs[g]
            sp = jnp.maximum(z, 0.0) + jnp.log2(1.0 + jnp.exp2(-jnp.abs(z)))
            if valid2 is not None:
                sp = masked(sp, valid2)
            rs.append(_dot(sp.astype(BF16), tri))
            row_sums.append([jnp.sum(sp[:, h * BLOCK:(h + 1) * BLOCK], axis=-1, keepdims=True)
                             for h in range(2)])
            if ahead is not None:
                slot, load_kt, ahead_rows = ahead
                zbuf_ref[slot, g, ahead_rows] = pair_scores(g, ahead_rows, load_kt)
        carry_max = None
        for g in range(N_PAIRS_B):
            a_heads = []
            for h in range(2):
                keys = slice(h * BLOCK, (h + 1) * BLOCK)
                carry = carry_ref[g, h, rows]
                a = jnp.exp2(zs[g][:, keys] + rs[g][:, keys] + carry)
                if valid is not None:
                    a = masked(a, valid)
                carry = carry - row_sums[g][h]
                carry_ref[g, h, rows] = carry
                if carry_max_rows is not None:
                    watched = carry[carry_max_rows]
                    carry_max = watched if carry_max is None else jnp.maximum(carry_max, watched)
                a_heads.append(a.astype(BF16))
            oacc_ref[rows, pair_lanes(g)] += _dot(jnp.concatenate(a_heads, axis=1),
                                                  jnp.concatenate(load_v(pair_lanes(g)), axis=0))
        return None if carry_max is None else jnp.max(carry_max)

    def real_kt(kb):
        return lambda sl: (ktlo_ref[0, kb, sl, :], kthi_ref[0, kb, sl, :])

    def real_v(kb):
        r0 = row_start(kb)
        return lambda sl: (vblo_ref[0, pl.ds(r0, BLOCK), sl], vbhi_ref[0, pl.ds(r0, BLOCK), sl])

    def stashed(slot, rows):
        return [zbuf_ref[slot, g, rows] for g in range(N_PAIRS_B)]

    all_rows = slice(0, Q_TILE)
    upper_rows = slice(BLOCK, Q_TILE)
    n_full = qi * blocks_per_tile
    top = n_full + 1
    meta_kt = lambda sl: (mktlo_ref[0, sl, :], mkthi_ref[0, sl, :])
    meta_v = lambda sl: (mvblo_ref[:, sl], mvbhi_ref[:, sl])
    lower_rows = slice(0, BLOCK)
    b_step(scores(upper_rows, real_kt(top)), real_v(top), k_minus_q[upper_rows] < -BLOCK,
           rows=upper_rows, ahead=(0, real_kt(top - 1), all_rows))
    young = slice(0, YOUNG_ROWS)
    middle = slice(YOUNG_ROWS, BLOCK + YOUNG_ROWS)
    old = slice(BLOCK + YOUNG_ROWS, Q_TILE)
    young_middle = slice(0, BLOCK + YOUNG_ROWS)
    every_row = slice(None)
    old_max = b_step(stashed(0, all_rows), real_v(top - 1), k_minus_q[lower_rows] < 0,
                     carry_max_rows=None if first_tile else old,
                     ahead=(1, meta_kt, all_rows) if first_tile else (1, real_kt(n_full - 1), young_middle))

    if first_tile:
        b_step(stashed(1, all_rows), meta_v, meta_valid)
    else:
        middle_max = b_step(stashed(1, young_middle), real_v(n_full - 1), None, rows=young_middle,
                            carry_max_rows=slice(YOUNG_ROWS, BLOCK + YOUNG_ROWS),
                            ahead=(0, real_kt(n_full - 2), young))
        young_max = b_step(stashed(0, young), real_v(n_full - 2), None, rows=young,
                           carry_max_rows=every_row)

        def alive(*carry_maxes):
            return (functools.reduce(jnp.maximum, carry_maxes) >= DEAD_LOG2).astype(jnp.int32)

        def tail_cond(state):
            j, live = state
            return jnp.logical_and(j < n_full, live > 0)

        def tail_body(state):
            j, _ = state
            kb_old = n_full - 1 - j
            maxes = []
            for lag, group in enumerate((old, middle, young)):
                kb = jnp.maximum(kb_old - lag, 0)
                exists = None if lag == 0 else (
                    k_minus_q[group] < jnp.where(kb_old >= lag, Q_TILE, -Q_TILE))
                maxes.append(b_step(scores(group, real_kt(kb)), real_v(kb), exists, rows=group,
                                    carry_max_rows=every_row))
            return j + 1, alive(*maxes)

        _, live = lax.while_loop(tail_cond, tail_body, (jnp.int32(0), alive(old_max, middle_max, young_max)))

        @pl.when(live > 0)
        def _():
            b_step(scores(all_rows, meta_kt), meta_v, meta_valid)

    out_ref[0, :, WIDTH_A:] = _rms(oacc_ref[...], gb_ref[...]).astype(out_ref.dtype)


def _attention(sinks, real, meta, gain_a, gain_b, tri, batch, seq):
    nq = seq // Q_TILE
    nkb = seq // BLOCK
    qa, katlo, kathi, valo, vahi, qb, ktlo, kthi, vblo, vbhi = real
    rows3 = lambda t: t.reshape(batch, seq, t.shape[-1])
    qa, valo, vahi, qb, vblo, vbhi = [rows3(t) for t in (qa, valo, vahi, qb, vblo, vbhi)]
    slabs = lambda t: t.reshape(batch, nkb, t.shape[-2], BLOCK)
    katlo, kathi, ktlo, kthi = [slabs(t) for t in (katlo, kathi, ktlo, kthi)]
    _, mkatlo, mkathi, mvalo, mvahi, _, mktlo, mkthi, mvblo, mvbhi = meta
    pairs = range(KV_DUP_A // LANES)
    pair = lambda c: slice(c * LANES, (c + 1) * LANES)
    mwa = jnp.stack([jnp.pad(jnp.concatenate([mkatlo[0, pair(c), PAD:], mkathi[0, pair(c), PAD:]], axis=1),
                             ((0, 0), (0, LANES - 2 * N_META))) for c in pairs])
    mva = jnp.stack([jnp.pad(jnp.concatenate([mvalo[PAD:, pair(c)], mvahi[PAD:, pair(c)]], axis=0),
                             ((0, LANES - 2 * N_META), (0, 0))) for c in pairs])
    lower_lane = jnp.arange(LANES)[None, :] < HEAD_DIM
    head_ones = lambda upper_row: (lower_lane != upper_row[:, None]).astype(BF16)
    ones_a = head_ones((jnp.arange(4 * BLOCK) // BLOCK) % 2 == 1)
    meta_row = jnp.arange(LANES)
    meta_ones = jnp.where((meta_row < 2 * N_META)[:, None], head_ones(meta_row >= N_META), 0).astype(BF16)
    mva = jnp.concatenate([mva, jnp.broadcast_to(meta_ones, mva.shape)], axis=2)
    qspec = lambda w: pl.BlockSpec((1, Q_TILE, w), lambda b, i: (b, i, 0))
    seqspec = lambda w: pl.BlockSpec((1, seq, w), lambda b, i: (b, 0, 0))
    ktspec = lambda w: pl.BlockSpec((1, nkb, w, BLOCK), lambda b, i: (b, 0, 0, 0))
    const = lambda a: pl.BlockSpec(a.shape, lambda b, i: (0,) * a.ndim)
    mix = WIDTH_A + WIDTH_B
    return pl.pallas_call(
        _attention_kernel,
        grid=(batch, nq),
        in_specs=[pl.BlockSpec(memory_space=pltpu.SMEM),
                  qspec(WIDTH_A), ktspec(KV_DUP_A), ktspec(KV_DUP_A), seqspec(KV_DUP_A), seqspec(KV_DUP_A),
                  qspec(WIDTH_B), ktspec(WIDTH_B), ktspec(WIDTH_B), seqspec(WIDTH_B), seqspec(WIDTH_B),
                  const(mwa), const(mva), const(ones_a), const(mktlo), const(mkthi),
                  const(mvblo), const(mvbhi), const(gain_a), const(gain_b), const(tri)],
        out_specs=pl.BlockSpec((1, Q_TILE, mix), lambda b, i: (b, i, 0)),
        out_shape=jax.ShapeDtypeStruct((batch, seq, mix), BF16),
        scratch_shapes=[pltpu.VMEM((N_PAIRS_B, 2, Q_TILE, LANES), F32),
                        pltpu.VMEM((Q_TILE, WIDTH_B), F32),
                        pltpu.VMEM((2, N_PAIRS_B, Q_TILE, 2 * BLOCK), F32)],
        compiler_params=pltpu.CompilerParams(
            dimension_semantics=("arbitrary", "arbitrary"), vmem_limit_bytes=VMEM_LIMIT),
        name="attention",
    )(sinks, qa, katlo, kathi, valo, vahi, qb, ktlo, kthi, vblo, vbhi, mwa, mva, ones_a, mktlo, mkthi,
      mvblo, mvbhi, gain_a, gain_b, tri)


def _out_ffn_kernel(ff_chunks, n_sub, mixed_ref, x_ref, wo_ref, gf_ref, wg_ref, wu_ref, wd_ref, gl_ref,
                    out_ref):
    sub = x_ref.shape[0] // n_sub
    rows = [slice(i * sub, (i + 1) * sub) for i in range(n_sub)]
    hs = [x_ref[r, :] + _dot(mixed_ref[r, :], wo_ref[...]) for r in rows]
    for r, h in zip(rows, hs):
        u = _rms(h, gf_ref[...]).astype(BF16)
        y = h
        for c0, c1 in ff_chunks:
            gate = _dot(u, wg_ref[:, c0:c1])
            up = _dot(u, wu_ref[:, c0:c1])
            act = (gate * jax.nn.sigmoid(gate) * up).astype(BF16)
            y = y + _dot(act, wd_ref[c0:c1, :])
        out_ref[r, :] = _rms(y, gl_ref[...])


def _ff_chunks(d_ff, n_chunks, col_tile):
    tiles = -(-d_ff // col_tile)
    bounds = [min(d_ff, (tiles * k // n_chunks) * col_tile) for k in range(n_chunks + 1)]
    return tuple((bounds[k], bounds[k + 1]) for k in range(n_chunks))


def _out_ffn(mixed2d, x2d, w_out, g_ffn, w_gate, w_up, w_down, g_final, row_tile):
    n, d = x2d.shape
    d_ff = w_gate.shape[1]
    row_spec = lambda w: pl.BlockSpec((row_tile, w), lambda i: (i, 0))
    const = lambda a: pl.BlockSpec(a.shape, lambda i: (0, 0), pipeline_mode=pl.Buffered(1))
    return pl.pallas_call(
        functools.partial(_out_ffn_kernel, _ff_chunks(d_ff, 2, V7X_MXU_COLS), FFN_SUB_TILES),
        grid=(n // row_tile,),
        in_specs=[row_spec(mixed2d.shape[1]), row_spec(d), const(w_out), const(g_ffn),
                  const(w_gate), const(w_up), const(w_down), const(g_final)],
        out_specs=row_spec(d),
        out_shape=jax.ShapeDtypeStruct((n, d), x2d.dtype),
        compiler_params=pltpu.CompilerParams(
            dimension_semantics=("arbitrary",), vmem_limit_bytes=VMEM_LIMIT),
        name="out_ffn",
    )(mixed2d, x2d, w_out, g_ffn, w_gate, w_up, w_down, g_final)


def _rope_tables(pos):
    half = HEAD_DIM // 2
    inv_freq = ROPE_THETA ** (-jnp.arange(half, dtype=F32) / half)
    ang = pos.astype(F32)[:, None] * inv_freq[None, :]
    cos, sin = jnp.cos(ang), jnp.sin(ang)
    reps = LANES // HEAD_DIM
    return (jnp.tile(jnp.concatenate([cos, cos], axis=1), (1, reps)),
            jnp.tile(jnp.concatenate([-sin, sin], axis=1), (1, reps)), cos.T, sin.T)


def _rearranged_w_in(w):
    o_ka = WIDTH_A
    o_va = o_ka + KV_WIDTH_A
    o_qb = o_va + KV_WIDTH_A
    o_kb = o_qb + WIDTH_B
    o_vb = o_kb + WIDTH_B
    dup = lambda t: jnp.repeat(t.reshape(t.shape[0], N_KV_A, 1, HEAD_DIM), 2, axis=2).reshape(t.shape[0], KV_DUP_A)
    w_cat = jnp.concatenate([w[:, :o_ka], dup(w[:, o_va:o_qb]), w[:, o_qb:o_kb], w[:, o_vb:]],
                            axis=1).astype(BF16)
    w_kt = jnp.concatenate([w[:, o_ka:o_va], w[:, o_kb:o_vb]], axis=1).T.astype(BF16)
    return w_cat, w_kt


def kernel(x, meta_tokens, norm_mix, w_in, sinks, norm_out_a, norm_out_b, w_out,
           norm_ffn, w_gate, w_up, w_down, norm_final):
    batch, seq, d = x.shape
    assert norm_mix.shape[0] == 1, "single-layer block: meta rows are not carried past the mixer"
    assert seq % ROW_TILE_PROJ == 0 and (batch * seq) % ROW_TILE_FFN == 0 and seq % Q_TILE == 0
    x2d = x.reshape(batch * seq, d)
    w_cat, w_kt = _rearranged_w_in(w_in[0])
    assert w_cat.shape[1] == _C_END
    g_mix = norm_mix[0].reshape(1, d)

    meta_block = jnp.concatenate([jnp.zeros((PAD, d), x.dtype), meta_tokens.astype(x.dtype)], axis=0)
    real = _in_proj(x2d, g_mix, w_cat, w_kt, _rope_tables(jnp.arange(seq) + N_META), ROW_TILE_PROJ)
    meta = _in_proj(meta_block, g_mix, w_cat, w_kt, _rope_tables(jnp.arange(BLOCK) - PAD), BLOCK)

    j = jnp.arange(2 * BLOCK)[:, None]
    s = jnp.arange(2 * BLOCK)[None, :]
    tri = jnp.where((j // BLOCK == s // BLOCK) & (j >= s), -1.0, 0.0).astype(BF16)

    mixed = _attention(sinks[0], real, meta, norm_out_a[0].reshape(1, WIDTH_A),
                       norm_out_b[0].reshape(1, WIDTH_B), tri, batch, seq)

    out = _out_ffn(mixed.reshape(batch * seq, WIDTH_A + WIDTH_B), x2d, w_out[0].astype(BF16),
                   norm_ffn[0].reshape(1, d), w_gate[0].astype(BF16), w_up[0].astype(BF16),
                   w_down[0].astype(BF16), norm_final.reshape(1, d), ROW_TILE_FFN)
    return out.reshape(batch, seq, d)
```

```python
import functools

import jax
import jax.numpy as jnp
from jax import lax
from jax.experimental import pallas as pl
from jax.experimental.pallas import tpu as pltpu

HEAD_DIM = 64
N_HEADS_A = 8
N_KV_A = 2
N_HEADS_B = 8
WIDTH_A = N_HEADS_A * HEAD_DIM
KV_WIDTH_A = N_KV_A * HEAD_DIM
WIDTH_B = N_HEADS_B * HEAD_DIM
BLOCK = 128
N_META = 16
PAD = BLOCK - N_META
ROPE_THETA = 10000.0
EPS = 1e-6
LANES = 128
V7X_MXU_COLS = 256
V7X_VMEM_BYTES = 64 * 1024 * 1024
N_PAIRS_A = WIDTH_A // LANES
N_PAIRS_B = WIDTH_B // LANES
KV_DUP_A = 2 * KV_WIDTH_A

Q_TILE = 2 * BLOCK
TILES_PER_STEP = 4
YOUNG_ROWS = 32
ROW_TILE_PROJ = 1024
PROJ_SUB_ROWS = 256
ROW_TILE_FFN = 1024
FFN_SUB_TILES = 4
VMEM_LIMIT = V7X_VMEM_BYTES * 7 // 8

F32 = jnp.float32
BF16 = jnp.bfloat16
LOG2_E = 1.4426950408889634
DEAD_LOG2 = -150.0

_C_QA = 0
_C_VA = _C_QA + WIDTH_A
_C_QB = _C_VA + KV_DUP_A
_C_VB = _C_QB + WIDTH_B
_C_END = _C_VB + WIDTH_B


def _rms(xf, g):
    ms = jnp.mean(xf * xf, axis=-1, keepdims=True)
    return xf * lax.rsqrt(ms + EPS) * g


def _dot(a, b):
    return jnp.dot(a, b, preferred_element_type=F32)


def _dot_nt(a, b):
    return lax.dot_general(a, b, (((1,), (1,)), ((), ())), preferred_element_type=F32)


def _in_proj_kernel(sub_rows, *refs):
    for r0 in range(0, refs[0].shape[0], sub_rows):
        _in_proj_rows(slice(r0, r0 + sub_rows), r0 // BLOCK, *refs)


def _in_proj_rows(rows, kb0, x_ref, g_ref, w_ref, wkt_ref, cos_ref, sin_ref, cosf_ref, sinf_ref,
                  qa_ref, katlo_ref, kathi_ref, valo_ref, vahi_ref,
                  qb_ref, ktlo_ref, kthi_ref, vblo_ref, vbhi_ref):
    tm = rows.stop - rows.start
    u = _rms(x_ref[rows, :], g_ref[...]).astype(BF16)
    proj = _dot(u, w_ref[...])
    cos = cos_ref[rows, :]
    sin = sin_ref[rows, :]
    lane = lax.broadcasted_iota(jnp.int32, (tm, LANES), 1)
    first_half = (lane & (HEAD_DIM // 2)) == 0
    lower_head = lane < HEAD_DIM
    scale = HEAD_DIM ** -0.5

    def rope(t):
        rot = jnp.where(first_half,
                        pltpu.roll(t, LANES - HEAD_DIM // 2, 1),
                        pltpu.roll(t, HEAD_DIM // 2, 1))
        return t * cos + rot * sin

    def group(c0, j):
        return proj[:, c0 + j * LANES:c0 + (j + 1) * LANES]

    for j in range(N_PAIRS_A):
        sl = slice(j * LANES, (j + 1) * LANES)
        qa_ref[rows, sl] = (rope(group(_C_QA, j)) * (scale * LOG2_E)).astype(BF16)
    for j in range(KV_DUP_A // LANES):
        sl = slice(j * LANES, (j + 1) * LANES)
        v = group(_C_VA, j)
        valo_ref[rows, sl] = jnp.where(lower_head, v, 0.0).astype(BF16)
        vahi_ref[rows, sl] = jnp.where(lower_head, 0.0, v).astype(BF16)
    for j in range(N_PAIRS_B):
        sl = slice(j * LANES, (j + 1) * LANES)
        qb_ref[rows, sl] = (group(_C_QB, j) * (scale * LOG2_E)).astype(BF16)
        v = group(_C_VB, j)
        vblo_ref[rows, sl] = jnp.where(lower_head, v, 0.0).astype(BF16)
        vbhi_ref[rows, sl] = jnp.where(lower_head, 0.0, v).astype(BF16)

    kt_all = _dot_nt(wkt_ref[...], u)

    cos_f = cosf_ref[:, rows]
    sin_f = sinf_ref[:, rows]
    half = HEAD_DIM // 2
    zeros = jnp.zeros((HEAD_DIM, tm), F32)
    lo_parts, hi_parts = [], []
    for c in range(N_KV_A):
        x1 = kt_all[c * HEAD_DIM:c * HEAD_DIM + half]
        x2 = kt_all[c * HEAD_DIM + half:(c + 1) * HEAD_DIM]
        roped = jnp.concatenate([x1 * cos_f - x2 * sin_f, x2 * cos_f + x1 * sin_f], axis=0)
        lo_parts += [roped, zeros]
        hi_parts += [zeros, roped]
    kat_lo = jnp.concatenate(lo_parts, axis=0).astype(BF16)
    kat_hi = jnp.concatenate(hi_parts, axis=0).astype(BF16)

    kt = kt_all[KV_WIDTH_A:]
    lower_feat = (lax.broadcasted_iota(jnp.int32, kt.shape, 0) & HEAD_DIM) == 0
    kt_lo = jnp.where(lower_feat, kt, 0.0).astype(BF16)
    kt_hi = jnp.where(lower_feat, 0.0, kt).astype(BF16)
    for j in range(tm // BLOCK):
        cols = slice(j * BLOCK, (j + 1) * BLOCK)
        katlo_ref[kb0 + j] = kat_lo[:, cols]
        kathi_ref[kb0 + j] = kat_hi[:, cols]
        ktlo_ref[kb0 + j] = kt_lo[:, cols]
        kthi_ref[kb0 + j] = kt_hi[:, cols]


def _in_proj(x2d, gain, w_cat, w_kt, tables, row_tile):
    n, d = x2d.shape
    cos_t, sin_t, cos_f, sin_f = tables
    table_tiles = cos_t.shape[0] // row_tile
    kb_per_tile = row_tile // BLOCK
    row_spec = lambda w: pl.BlockSpec((row_tile, w), lambda i: (i, 0))
    const_spec = lambda shape: pl.BlockSpec(shape, lambda i: (0, 0), pipeline_mode=pl.Buffered(1))
    table_spec = pl.BlockSpec((row_tile, LANES), lambda i: (i % table_tiles, 0))
    ftable_spec = pl.BlockSpec((HEAD_DIM // 2, row_tile), lambda i: (0, i % table_tiles))
    row_out = lambda w: (row_spec(w), jax.ShapeDtypeStruct((n, w), BF16))
    kt_out = lambda w: (pl.BlockSpec((kb_per_tile, w, BLOCK), lambda i: (i, 0, 0)),
                        jax.ShapeDtypeStruct((n // BLOCK, w, BLOCK), BF16))
    outs = [row_out(WIDTH_A), kt_out(KV_DUP_A), kt_out(KV_DUP_A), row_out(KV_DUP_A), row_out(KV_DUP_A),
            row_out(WIDTH_B), kt_out(WIDTH_B), kt_out(WIDTH_B), row_out(WIDTH_B), row_out(WIDTH_B)]
    return pl.pallas_call(
        functools.partial(_in_proj_kernel, min(row_tile, PROJ_SUB_ROWS)),
        grid=(n // row_tile,),
        in_specs=[row_spec(d), const_spec((1, d)), const_spec(w_cat.shape), const_spec(w_kt.shape),
                  table_spec, table_spec, ftable_spec, ftable_spec],
        out_specs=[o[0] for o in outs],
        out_shape=[o[1] for o in outs],
        compiler_params=pltpu.CompilerParams(
            dimension_semantics=("arbitrary",), vmem_limit_bytes=VMEM_LIMIT),
        name="in_proj",
    )(x2d, gain, w_cat, w_kt, cos_t, sin_t, cos_f, sin_f)


def _attention_kernel(*refs):
    step = pl.program_id(1)

    @pl.when(step == 0)
    def _():
        for t in range(TILES_PER_STEP):
            _attention_tile(t, t * Q_TILE, *refs)

    @pl.when(step > 0)
    def _():
        for t in range(TILES_PER_STEP):
            _attention_tile(step * TILES_PER_STEP + t, t * Q_TILE, *refs)


def _attention_tile(qi, row0, sinks_ref, qa_ref, katlo_ref, kathi_ref, valo_ref, vahi_ref,
                    qb_ref, ktlo_ref, kthi_ref, vblo_ref, vbhi_ref,
                    mwa_ref, mva_ref, onesa_ref, mktlo_ref, mkthi_ref, mvblo_ref, mvbhi_ref,
                    ga_ref, gb_ref, tri_ref, out_ref, carry_ref, oacc_ref, zbuf_ref):
    first_tile = isinstance(qi, int) and qi == 0
    blocks_per_tile = Q_TILE // BLOCK
    assert blocks_per_tile == 2, "the group-B sweep peels exactly two diagonal key blocks"

    def row_start(blk):
        return blk * BLOCK if isinstance(blk, int) else pl.multiple_of(blk * BLOCK, BLOCK)

    in_cur = ((lax.broadcasted_iota(jnp.int32, (BLOCK, 2 * BLOCK), 1) & (BLOCK - 1))
              <= lax.broadcasted_iota(jnp.int32, (BLOCK, 2 * BLOCK), 0))
    in_cur1 = in_cur[:, :BLOCK]
    lane = lax.broadcasted_iota(jnp.int32, (BLOCK, LANES), 1)
    meta_lo = lane < N_META
    meta_any = lane < 2 * N_META
    lower_lanes = lane < HEAD_DIM
    neg_inf = -jnp.inf
    chains = [(half, g) for half in range(blocks_per_tile) for g in range(N_PAIRS_A)]

    def a_blocks(half):
        blk = qi * blocks_per_tile + half
        return blk, (None if first_tile and half == 0 else blk - 1)

    def a_keys(blk, csl):
        return jnp.concatenate([katlo_ref[0, blk, csl, :], kathi_ref[0, blk, csl, :]], axis=1)

    scores = []
    for half, g in chains:
        blk, prev = a_blocks(half)
        c = (2 * g) // (N_HEADS_A // N_KV_A)
        csl = slice(c * LANES, (c + 1) * LANES)
        q2 = qa_ref[0, row0 + half * BLOCK:row0 + (half + 1) * BLOCK, g * LANES:(g + 1) * LANES]
        s_prev = neg_inf if prev is None else _dot(q2, a_keys(prev, csl))
        scores.append((_dot(q2, a_keys(blk, csl)), s_prev, _dot(q2, mwa_ref[c])))
    probs = []
    for (half, g), (s_cur, s_prev, sm) in zip(chains, scores):
        s = jnp.where(in_cur, s_cur, s_prev)
        s_lo, s_hi = s[:, :BLOCK], s[:, BLOCK:]
        sink_lo, sink_hi = sinks_ref[2 * g] * LOG2_E, sinks_ref[2 * g + 1] * LOG2_E
        m_lo = jnp.maximum(jnp.max(jnp.maximum(s_lo, jnp.where(meta_lo, sm, neg_inf)),
                                   axis=-1, keepdims=True), sink_lo)
        m_hi = jnp.maximum(jnp.max(jnp.maximum(s_hi, jnp.where(meta_lo | ~meta_any, neg_inf, sm)),
                                   axis=-1, keepdims=True), sink_hi)
        p_lo = jnp.exp2(s_lo - m_lo)
        p_hi = jnp.exp2(s_hi - m_hi)
        pm = jnp.where(meta_any, jnp.exp2(sm - jnp.where(meta_lo, m_lo, m_hi)), 0.0)
        pcat = jnp.concatenate([jnp.where(in_cur1, 0.0, p_lo), jnp.where(in_cur1, 0.0, p_hi),
                                jnp.where(in_cur1, p_lo, 0.0), jnp.where(in_cur1, p_hi, 0.0)],
                               axis=1).astype(BF16)
        sink_term = jnp.where(lower_lanes, jnp.exp2(sink_lo - m_lo), jnp.exp2(sink_hi - m_hi))
        probs.append((pcat, pm.astype(BF16), sink_term))
    ones_a = onesa_ref[...]
    for half in range(blocks_per_tile):
        blk, prev = a_blocks(half)
        cur0 = row_start(blk)
        prev0 = cur0 if prev is None else row_start(prev)
        oa = []
        for g in range(N_PAIRS_A):
            c = (2 * g) // (N_HEADS_A // N_KV_A)
            csl = slice(c * LANES, (c + 1) * LANES)
            pcat, pm, sink_term = probs[half * N_PAIRS_A + g]
            vcat = jnp.concatenate(
                [valo_ref[0, pl.ds(prev0, BLOCK), csl], vahi_ref[0, pl.ds(prev0, BLOCK), csl],
                 valo_ref[0, pl.ds(cur0, BLOCK), csl], vahi_ref[0, pl.ds(cur0, BLOCK), csl]], axis=0)
            res = _dot(pcat, jnp.concatenate([vcat, ones_a], axis=1)) + _dot(pm, mva_ref[c])
            oa.append(res[:, :LANES] / (res[:, LANES:] + sink_term))
        oa = jnp.concatenate(oa, axis=1)
        out_ref[0, row0 + half * BLOCK:row0 + (half + 1) * BLOCK, :WIDTH_A] = (
            _rms(oa, ga_ref[...]).astype(out_ref.dtype))

    carry_ref[...] = jnp.zeros_like(carry_ref)
    oacc_ref[...] = jnp.zeros_like(oacc_ref)
    tri = tri_ref[...]
    k_minus_q = (lax.broadcasted_iota(jnp.int32, (Q_TILE, LANES), 1)
                 - lax.broadcasted_iota(jnp.int32, (Q_TILE, LANES), 0))
    meta_valid = lax.broadcasted_iota(jnp.int32, (Q_TILE, LANES), 1) >= PAD

    def pair_lanes(g):
        return slice(g * LANES, (g + 1) * LANES)

    def pair_scores(g, rows, load_kt):
        q_rows = slice(row0 + rows.start, row0 + rows.stop)
        return _dot(qb_ref[0, q_rows, pair_lanes(g)], jnp.concatenate(load_kt(pair_lanes(g)), axis=1))

    def scores(rows, load_kt):
        return [pair_scores(g, rows, load_kt) for g in range(N_PAIRS_B)]

    def masked(t, valid):
        n = valid.shape[0]
        head = jnp.where(valid, t[:n], 0.0)
        return head if n == t.shape[0] else jnp.concatenate([head, t[n:]], axis=0)

    def b_step(zs, load_v, valid, rows=slice(0, Q_TILE), carry_max_rows=None, ahead=None):
        rs, row_sums = [], []
        valid2 = None if valid is None else jnp.concatenate([valid, valid], axis=1)
        for g in range(N_PAIRS_B):
            z = zs[g]
            sp = jnp.maximum(z, 0.0) + jnp.log2(1.0 + jnp.exp2(-jnp.abs(z)))
            if valid2 is not None:
                sp = masked(sp, valid2)
            rs.append(_dot(sp.astype(BF16), tri))
            row_sums.append([jnp.sum(sp[:, h * BLOCK:(h + 1) * BLOCK], axis=-1, keepdims=True)
                             for h in range(2)])
            if ahead is not None:
                slot, load_kt, ahead_rows = ahead
                zbuf_ref[slot, g, ahead_rows] = pair_scores(g, ahead_rows, load_kt)
        carry_max = None
        for g in range(N_PAIRS_B):
            a_heads = []
            for h in range(2):
                keys = slice(h * BLOCK, (h + 1) * BLOCK)
                carry = carry_ref[g, h, rows]
                a = jnp.exp2(zs[g][:, keys] + rs[g][:, keys] + carry)
                if valid is not None:
                    a = masked(a, valid)
                carry = carry - row_sums[g][h]
                carry_ref[g, h, rows] = carry
                if carry_max_rows is not None:
                    watched = carry[carry_max_rows]
                    carry_max = watched if carry_max is None else jnp.maximum(carry_max, watched)
                a_heads.append(a.astype(BF16))
            oacc_ref[rows, pair_lanes(g)] += _dot(jnp.concatenate(a_heads, axis=1),
                                                  jnp.concatenate(load_v(pair_lanes(g)), axis=0))
        return None if carry_max is None else jnp.max(carry_max)

    def real_kt(kb):
        return lambda sl: (ktlo_ref[0, kb, sl, :], kthi_ref[0, kb, sl, :])

    def real_v(kb):
        r0 = row_start(kb)
        return lambda sl: (vblo_ref[0, pl.ds(r0, BLOCK), sl], vbhi_ref[0, pl.ds(r0, BLOCK), sl])

    def stashed(slot, rows):
        return [zbuf_ref[slot, g, rows] for g in range(N_PAIRS_B)]

    all_rows = slice(0, Q_TILE)
    upper_rows = slice(BLOCK, Q_TILE)
    n_full = qi * blocks_per_tile
    top = n_full + 1
    meta_kt = lambda sl: (mktlo_ref[0, sl, :], mkthi_ref[0, sl, :])
    meta_v = lambda sl: (mvblo_ref[:, sl], mvbhi_ref[:, sl])
    lower_rows = slice(0, BLOCK)
    b_step(scores(upper_rows, real_kt(top)), real_v(top), k_minus_q[upper_rows] < -BLOCK,
           rows=upper_rows, ahead=(0, real_kt(top - 1), all_rows))
    young = slice(0, YOUNG_ROWS)
    middle = slice(YOUNG_ROWS, BLOCK + YOUNG_ROWS)
    old = slice(BLOCK + YOUNG_ROWS, Q_TILE)
    young_middle = slice(0, BLOCK + YOUNG_ROWS)
    every_row = slice(None)
    old_max = b_step(stashed(0, all_rows), real_v(top - 1), k_minus_q[lower_rows] < 0,
                     carry_max_rows=None if first_tile else old,
                     ahead=(1, meta_kt, all_rows) if first_tile else (1, real_kt(n_full - 1), young_middle))

    if first_tile:
        b_step(stashed(1, all_rows), meta_v, meta_valid)
    else:
        middle_max = b_step(stashed(1, young_middle), real_v(n_full - 1), None, rows=young_middle,
                            carry_max_rows=slice(YOUNG_ROWS, BLOCK + YOUNG_ROWS),
                            ahead=(0, real_kt(n_full - 2), young))
        young_max = b_step(stashed(0, young), real_v(n_full - 2), None, rows=young,
                           carry_max_rows=every_row)

        def alive(*carry_maxes):
            return (functools.reduce(jnp.maximum, carry_maxes) >= DEAD_LOG2).astype(jnp.int32)

        def tail_cond(state):
            j, live = state
            return jnp.logical_and(j < n_full, live > 0)

        def tail_body(state):
            j, _ = state
            kb_old = n_full - 1 - j
            maxes = []
            for lag, group in enumerate((old, middle, young)):
                kb = jnp.maximum(kb_old - lag, 0)
                exists = None if lag == 0 else (
                    k_minus_q[group] < jnp.where(kb_old >= lag, Q_TILE, -Q_TILE))
                maxes.append(b_step(scores(group, real_kt(kb)), real_v(kb), exists, rows=group,
                                    carry_max_rows=every_row))
            return j + 1, alive(*maxes)

        _, live = lax.while_loop(tail_cond, tail_body, (jnp.int32(0), alive(old_max, middle_max, young_max)))

        @pl.when(live > 0)
        def _():
            b_step(scores(all_rows, meta_kt), meta_v, meta_valid)

    out_ref[0, row0:row0 + Q_TILE, WIDTH_A:] = _rms(oacc_ref[...], gb_ref[...]).astype(out_ref.dtype)


def _attention(sinks, real, meta, gain_a, gain_b, tri, batch, seq):
    nq = seq // Q_TILE
    nkb = seq // BLOCK
    qa, katlo, kathi, valo, vahi, qb, ktlo, kthi, vblo, vbhi = real
    rows3 = lambda t: t.reshape(batch, seq, t.shape[-1])
    qa, valo, vahi, qb, vblo, vbhi = [rows3(t) for t in (qa, valo, vahi, qb, vblo, vbhi)]
    slabs = lambda t: t.reshape(batch, nkb, t.shape[-2], BLOCK)
    katlo, kathi, ktlo, kthi = [slabs(t) for t in (katlo, kathi, ktlo, kthi)]
    _, mkatlo, mkathi, mvalo, mvahi, _, mktlo, mkthi, mvblo, mvbhi = meta
    pairs = range(KV_DUP_A // LANES)
    pair = lambda c: slice(c * LANES, (c + 1) * LANES)
    mwa = jnp.stack([jnp.pad(jnp.concatenate([mkatlo[0, pair(c), PAD:], mkathi[0, pair(c), PAD:]], axis=1),
                             ((0, 0), (0, LANES - 2 * N_META))) for c in pairs])
    mva = jnp.stack([jnp.pad(jnp.concatenate([mvalo[PAD:, pair(c)], mvahi[PAD:, pair(c)]], axis=0),
                             ((0, LANES - 2 * N_META), (0, 0))) for c in pairs])
    lower_lane = jnp.arange(LANES)[None, :] < HEAD_DIM
    head_ones = lambda upper_row: (lower_lane != upper_row[:, None]).astype(BF16)
    ones_a = head_ones((jnp.arange(4 * BLOCK) // BLOCK) % 2 == 1)
    meta_row = jnp.arange(LANES)
    meta_ones = jnp.where((meta_row < 2 * N_META)[:, None], head_ones(meta_row >= N_META), 0).astype(BF16)
    mva = jnp.concatenate([mva, jnp.broadcast_to(meta_ones, mva.shape)], axis=2)
    step_rows = TILES_PER_STEP * Q_TILE
    qspec = lambda w: pl.BlockSpec((1, step_rows, w), lambda b, i: (b, i, 0))
    seqspec = lambda w: pl.BlockSpec((1, seq, w), lambda b, i: (b, 0, 0))
    ktspec = lambda w: pl.BlockSpec((1, nkb, w, BLOCK), lambda b, i: (b, 0, 0, 0))
    const = lambda a: pl.BlockSpec(a.shape, lambda b, i: (0,) * a.ndim)
    mix = WIDTH_A + WIDTH_B
    return pl.pallas_call(
        _attention_kernel,
        grid=(batch, nq // TILES_PER_STEP),
        in_specs=[pl.BlockSpec(memory_space=pltpu.SMEM),
                  qspec(WIDTH_A), ktspec(KV_DUP_A), ktspec(KV_DUP_A), seqspec(KV_DUP_A), seqspec(KV_DUP_A),
                  qspec(WIDTH_B), ktspec(WIDTH_B), ktspec(WIDTH_B), seqspec(WIDTH_B), seqspec(WIDTH_B),
                  const(mwa), const(mva), const(ones_a), const(mktlo), const(mkthi),
                  const(mvblo), const(mvbhi), const(gain_a), const(gain_b), const(tri)],
        out_specs=qspec(mix),
        out_shape=jax.ShapeDtypeStruct((batch, seq, mix), BF16),
        scratch_shapes=[pltpu.VMEM((N_PAIRS_B, 2, Q_TILE, LANES), F32),
                        pltpu.VMEM((Q_TILE, WIDTH_B), F32),
                        pltpu.VMEM((2, N_PAIRS_B, Q_TILE, 2 * BLOCK), F32)],
        compiler_params=pltpu.CompilerParams(
            dimension_semantics=("arbitrary", "arbitrary"), vmem_limit_bytes=VMEM_LIMIT),
        name="attention",
    )(sinks, qa, katlo, kathi, valo, vahi, qb, ktlo, kthi, vblo, vbhi, mwa, mva, ones_a, mktlo, mkthi,
      mvblo, mvbhi, gain_a, gain_b, tri)


def _out_ffn_kernel(ff_chunks, n_sub, mixed_ref, x_ref, wo_ref, gf_ref, wg_ref, wu_ref, wd_ref, gl_ref,
                    out_ref):
    sub = x_ref.shape[0] // n_sub
    rows = [slice(i * sub, (i + 1) * sub) for i in range(n_sub)]
    hs = [x_ref[r, :] + _dot(mixed_ref[r, :], wo_ref[...]) for r in rows]
    for r, h in zip(rows, hs):
        u = _rms(h, gf_ref[...]).astype(BF16)
        y = h
        for c0, c1 in ff_chunks:
            gate = _dot(u, wg_ref[:, c0:c1])
            up = _dot(u, wu_ref[:, c0:c1])
            act = (gate * jax.nn.sigmoid(gate) * up).astype(BF16)
            y = y + _dot(act, wd_ref[c0:c1, :])
        out_ref[r, :] = _rms(y, gl_ref[...])


def _ff_chunks(d_ff, n_chunks, col_tile):
    tiles = -(-d_ff // col_tile)
    bounds = [min(d_ff, (tiles * k // n_chunks) * col_tile) for k in range(n_chunks + 1)]
    return tuple((bounds[k], bounds[k + 1]) for k in range(n_chunks))


def _out_ffn(mixed2d, x2d, w_out, g_ffn, w_gate, w_up, w_down, g_final, row_tile):
    n, d = x2d.shape
    d_ff = w_gate.shape[1]
    row_spec = lambda w: pl.BlockSpec((row_tile, w), lambda i: (i, 0))
    const = lambda a: pl.BlockSpec(a.shape, lambda i: (0, 0), pipeline_mode=pl.Buffered(1))
    return pl.pallas_call(
        functools.partial(_out_ffn_kernel, _ff_chunks(d_ff, 2, V7X_MXU_COLS), FFN_SUB_TILES),
        grid=(n // row_tile,),
        in_specs=[row_spec(mixed2d.shape[1]), row_spec(d), const(w_out), const(g_ffn),
                  const(w_gate), const(w_up), const(w_down), const(g_final)],
        out_specs=row_spec(d),
        out_shape=jax.ShapeDtypeStruct((n, d), x2d.dtype),
        compiler_params=pltpu.CompilerParams(
            dimension_semantics=("arbitrary",), vmem_limit_bytes=VMEM_LIMIT),
        name="out_ffn",
    )(mixed2d, x2d, w_out, g_ffn, w_gate, w_up, w_down, g_final)


def _rope_tables(pos):
    half = HEAD_DIM // 2
    inv_freq = ROPE_THETA ** (-jnp.arange(half, dtype=F32) / half)
    ang = pos.astype(F32)[:, None] * inv_freq[None, :]
    cos, sin = jnp.cos(ang), jnp.sin(ang)
    reps = LANES // HEAD_DIM
    return (jnp.tile(jnp.concatenate([cos, cos], axis=1), (1, reps)),
            jnp.tile(jnp.concatenate([-sin, sin], axis=1), (1, reps)), cos.T, sin.T)


def _rearranged_w_in(w):
    o_ka = WIDTH_A
    o_va = o_ka + KV_WIDTH_A
    o_qb = o_va + KV_WIDTH_A
    o_kb = o_qb + WIDTH_B
    o_vb = o_kb + WIDTH_B
    dup = lambda t: jnp.repeat(t.reshape(t.shape[0], N_KV_A, 1, HEAD_DIM), 2, axis=2).reshape(t.shape[0], KV_DUP_A)
    w_cat = jnp.concatenate([w[:, :o_ka], dup(w[:, o_va:o_qb]), w[:, o_qb:o_kb], w[:, o_vb:]],
                            axis=1).astype(BF16)
    w_kt = jnp.concatenate([w[:, o_ka:o_va], w[:, o_kb:o_vb]], axis=1).T.astype(BF16)
    return w_cat, w_kt


def kernel(x, meta_tokens, norm_mix, w_in, sinks, norm_out_a, norm_out_b, w_out,
           norm_ffn, w_gate, w_up, w_down, norm_final):
    batch, seq, d = x.shape
    assert norm_mix.shape[0] == 1, "single-layer block: meta rows are not carried past the mixer"
    assert seq % ROW_TILE_PROJ == 0 and (batch * seq) % ROW_TILE_FFN == 0 and seq % (TILES_PER_STEP * Q_TILE) == 0
    x2d = x.reshape(batch * seq, d)
    w_cat, w_kt = _rearranged_w_in(w_in[0])
    assert w_cat.shape[1] == _C_END
    g_mix = norm_mix[0].reshape(1, d)

    meta_block = jnp.concatenate([jnp.zeros((PAD, d), x.dtype), meta_tokens.astype(x.dtype)], axis=0)
    real = _in_proj(x2d, g_mix, w_cat, w_kt, _rope_tables(jnp.arange(seq) + N_META), ROW_TILE_PROJ)
    meta = _in_proj(meta_block, g_mix, w_cat, w_kt, _rope_tables(jnp.arange(BLOCK) - PAD), BLOCK)

    j = jnp.arange(2 * BLOCK)[:, None]
    s = jnp.arange(2 * BLOCK)[None, :]
    tri = jnp.where((j // BLOCK == s // BLOCK) & (j >= s), -1.0, 0.0).astype(BF16)

    mixed = _attention(sinks[0], real, meta, norm_out_a[0].reshape(1, WIDTH_A),
                       norm_out_b[0].reshape(1, WIDTH_B), tri, batch, seq)

    out = _out_ffn(mixed.reshape(batch * seq, WIDTH_A + WIDTH_B), x2d, w_out[0].astype(BF16),
                   norm_ffn[0].reshape(1, d), w_gate[0].astype(BF16), w_up[0].astype(BF16),
                   w_down[0].astype(BF16), norm_final.reshape(1, d), ROW_TILE_FFN)
    return out.reshape(batch, seq, d)
```

```python
import functools

import jax
import jax.numpy as jnp
from jax import lax
from jax.experimental import pallas as pl
from jax.experimental.pallas import tpu as pltpu

HEAD_DIM = 64
N_HEADS_A = 8
N_KV_A = 2
N_HEADS_B = 8
WIDTH_A = N_HEADS_A * HEAD_DIM
KV_WIDTH_A = N_KV_A * HEAD_DIM
WIDTH_B = N_HEADS_B * HEAD_DIM
BLOCK = 128
N_META = 16
PAD = BLOCK - N_META
ROPE_THETA = 10000.0
EPS = 1e-6
LANES = 128
V7X_MXU_COLS = 256
V7X_VMEM_BYTES = 64 * 1024 * 1024
N_PAIRS_A = WIDTH_A // LANES
N_PAIRS_B = WIDTH_B // LANES
KV_DUP_A = 2 * KV_WIDTH_A

Q_TILE = 2 * BLOCK
TILES_PER_STEP = 8
YOUNG_ROWS = 32
ROW_TILE_PROJ = 1024
PROJ_SUB_ROWS = 256
ROW_TILE_FFN = 1024
FFN_SUB_TILES = 4
VMEM_LIMIT = V7X_VMEM_BYTES * 7 // 8

F32 = jnp.float32
BF16 = jnp.bfloat16
LOG2_E = 1.4426950408889634
DEAD_LOG2 = -150.0

_C_QA = 0
_C_VA = _C_QA + WIDTH_A
_C_QB = _C_VA + KV_DUP_A
_C_VB = _C_QB + WIDTH_B
_C_END = _C_VB + WIDTH_B


def _rms(xf, g):
    ms = jnp.mean(xf * xf, axis=-1, keepdims=True)
    return xf * lax.rsqrt(ms + EPS) * g


def _dot(a, b):
    return jnp.dot(a, b, preferred_element_type=F32)


def _dot_nt(a, b):
    return lax.dot_general(a, b, (((1,), (1,)), ((), ())), preferred_element_type=F32)


def _in_proj_kernel(sub_rows, *refs):
    for r0 in range(0, refs[0].shape[0], sub_rows):
        _in_proj_rows(slice(r0, r0 + sub_rows), r0 // BLOCK, *refs)


def _in_proj_rows(rows, kb0, x_ref, g_ref, w_ref, wkt_ref, cos_ref, sin_ref, cosf_ref, sinf_ref,
                  qa_ref, katlo_ref, kathi_ref, valo_ref, vahi_ref,
                  qb_ref, ktlo_ref, kthi_ref, vblo_ref, vbhi_ref):
    tm = rows.stop - rows.start
    u = _rms(x_ref[rows, :], g_ref[...]).astype(BF16)
    proj = _dot(u, w_ref[...])
    cos = cos_ref[rows, :]
    sin = sin_ref[rows, :]
    lane = lax.broadcasted_iota(jnp.int32, (tm, LANES), 1)
    first_half = (lane & (HEAD_DIM // 2)) == 0
    lower_head = lane < HEAD_DIM
    scale = HEAD_DIM ** -0.5

    def rope(t):
        rot = jnp.where(first_half,
                        pltpu.roll(t, LANES - HEAD_DIM // 2, 1),
                        pltpu.roll(t, HEAD_DIM // 2, 1))
        return t * cos + rot * sin

    def group(c0, j):
        return proj[:, c0 + j * LANES:c0 + (j + 1) * LANES]

    for j in range(N_PAIRS_A):
        sl = slice(j * LANES, (j + 1) * LANES)
        qa_ref[rows, sl] = (rope(group(_C_QA, j)) * (scale * LOG2_E)).astype(BF16)
    for j in range(KV_DUP_A // LANES):
        sl = slice(j * LANES, (j + 1) * LANES)
        v = group(_C_VA, j)
        valo_ref[rows, sl] = jnp.where(lower_head, v, 0.0).astype(BF16)
        vahi_ref[rows, sl] = jnp.where(lower_head, 0.0, v).astype(BF16)
    for j in range(N_PAIRS_B):
        sl = slice(j * LANES, (j + 1) * LANES)
        qb_ref[rows, sl] = (group(_C_QB, j) * (scale * LOG2_E)).astype(BF16)
        v = group(_C_VB, j)
        vblo_ref[rows, sl] = jnp.where(lower_head, v, 0.0).astype(BF16)
        vbhi_ref[rows, sl] = jnp.where(lower_head, 0.0, v).astype(BF16)

    kt_all = _dot_nt(wkt_ref[...], u)

    cos_f = cosf_ref[:, rows]
    sin_f = sinf_ref[:, rows]
    half = HEAD_DIM // 2
    zeros = jnp.zeros((HEAD_DIM, tm), F32)
    lo_parts, hi_parts = [], []
    for c in range(N_KV_A):
        x1 = kt_all[c * HEAD_DIM:c * HEAD_DIM + half]
        x2 = kt_all[c * HEAD_DIM + half:(c + 1) * HEAD_DIM]
        roped = jnp.concatenate([x1 * cos_f - x2 * sin_f, x2 * cos_f + x1 * sin_f], axis=0)
        lo_parts += [roped, zeros]
        hi_parts += [zeros, roped]
    kat_lo = jnp.concatenate(lo_parts, axis=0).astype(BF16)
    kat_hi = jnp.concatenate(hi_parts, axis=0).astype(BF16)

    kt = kt_all[KV_WIDTH_A:]
    lower_feat = (lax.broadcasted_iota(jnp.int32, kt.shape, 0) & HEAD_DIM) == 0
    kt_lo = jnp.where(lower_feat, kt, 0.0).astype(BF16)
    kt_hi = jnp.where(lower_feat, 0.0, kt).astype(BF16)
    for j in range(tm // BLOCK):
        cols = slice(j * BLOCK, (j + 1) * BLOCK)
        katlo_ref[kb0 + j] = kat_lo[:, cols]
        kathi_ref[kb0 + j] = kat_hi[:, cols]
        ktlo_ref[kb0 + j] = kt_lo[:, cols]
        kthi_ref[kb0 + j] = kt_hi[:, cols]


def _in_proj(x2d, gain, w_cat, w_kt, tables, row_tile):
    n, d = x2d.shape
    cos_t, sin_t, cos_f, sin_f = tables
    table_tiles = cos_t.shape[0] // row_tile
    kb_per_tile = row_tile // BLOCK
    row_spec = lambda w: pl.BlockSpec((row_tile, w), lambda i: (i, 0))
    const_spec = lambda shape: pl.BlockSpec(shape, lambda i: (0, 0), pipeline_mode=pl.Buffered(1))
    table_spec = pl.BlockSpec((row_tile, LANES), lambda i: (i % table_tiles, 0))
    ftable_spec = pl.BlockSpec((HEAD_DIM // 2, row_tile), lambda i: (0, i % table_tiles))
    row_out = lambda w: (row_spec(w), jax.ShapeDtypeStruct((n, w), BF16))
    kt_out = lambda w: (pl.BlockSpec((kb_per_tile, w, BLOCK), lambda i: (i, 0, 0)),
                        jax.ShapeDtypeStruct((n // BLOCK, w, BLOCK), BF16))
    outs = [row_out(WIDTH_A), kt_out(KV_DUP_A), kt_out(KV_DUP_A), row_out(KV_DUP_A), row_out(KV_DUP_A),
            row_out(WIDTH_B), kt_out(WIDTH_B), kt_out(WIDTH_B), row_out(WIDTH_B), row_out(WIDTH_B)]
    return pl.pallas_call(
        functools.partial(_in_proj_kernel, min(row_tile, PROJ_SUB_ROWS)),
        grid=(n // row_tile,),
        in_specs=[row_spec(d), const_spec((1, d)), const_spec(w_cat.shape), const_spec(w_kt.shape),
                  table_spec, table_spec, ftable_spec, ftable_spec],
        out_specs=[o[0] for o in outs],
        out_shape=[o[1] for o in outs],
        compiler_params=pltpu.CompilerParams(
            dimension_semantics=("arbitrary",), vmem_limit_bytes=VMEM_LIMIT),
        name="in_proj",
    )(x2d, gain, w_cat, w_kt, cos_t, sin_t, cos_f, sin_f)


def _attention_kernel(steps_per_seq, *refs):
    def first_step():
        for t in range(TILES_PER_STEP):
            _attention_tile(t, t * Q_TILE, *refs)

    if steps_per_seq == 1:
        first_step()
        return
    step = pl.program_id(1)
    pl.when(step == 0)(first_step)

    @pl.when(step > 0)
    def _():
        for t in range(TILES_PER_STEP):
            _attention_tile(step * TILES_PER_STEP + t, t * Q_TILE, *refs)


def _attention_tile(qi, row0, sinks_ref, qa_ref, katlo_ref, kathi_ref, valo_ref, vahi_ref,
                    qb_ref, ktlo_ref, kthi_ref, vblo_ref, vbhi_ref,
                    mwa_ref, mva_ref, onesa_ref, mktlo_ref, mkthi_ref, mvblo_ref, mvbhi_ref,
                    ga_ref, gb_ref, tri_ref, out_ref, carry_ref, oacc_ref, zbuf_ref):
    first_tile = isinstance(qi, int) and qi == 0
    blocks_per_tile = Q_TILE // BLOCK
    assert blocks_per_tile == 2, "the group-B sweep peels exactly two diagonal key blocks"

    def row_start(blk):
        return blk * BLOCK if isinstance(blk, int) else pl.multiple_of(blk * BLOCK, BLOCK)

    in_cur = ((lax.broadcasted_iota(jnp.int32, (BLOCK, 2 * BLOCK), 1) & (BLOCK - 1))
              <= lax.broadcasted_iota(jnp.int32, (BLOCK, 2 * BLOCK), 0))
    in_cur1 = in_cur[:, :BLOCK]
    lane = lax.broadcasted_iota(jnp.int32, (BLOCK, LANES), 1)
    meta_lo = lane < N_META
    meta_any = lane < 2 * N_META
    lower_lanes = lane < HEAD_DIM
    neg_inf = -jnp.inf
    chains = [(half, g) for half in range(blocks_per_tile) for g in range(N_PAIRS_A)]

    def a_blocks(half):
        blk = qi * blocks_per_tile + half
        return blk, (None if first_tile and half == 0 else blk - 1)

    def a_keys(blk, csl):
        return jnp.concatenate([katlo_ref[0, blk, csl, :], kathi_ref[0, blk, csl, :]], axis=1)

    scores = []
    for half, g in chains:
        blk, prev = a_blocks(half)
        c = (2 * g) // (N_HEADS_A // N_KV_A)
        csl = slice(c * LANES, (c + 1) * LANES)
        q2 = qa_ref[0, row0 + half * BLOCK:row0 + (half + 1) * BLOCK, g * LANES:(g + 1) * LANES]
        s_prev = neg_inf if prev is None else _dot(q2, a_keys(prev, csl))
        scores.append((_dot(q2, a_keys(blk, csl)), s_prev, _dot(q2, mwa_ref[c])))
    probs = []
    for (half, g), (s_cur, s_prev, sm) in zip(chains, scores):
        s = jnp.where(in_cur, s_cur, s_prev)
        s_lo, s_hi = s[:, :BLOCK], s[:, BLOCK:]
        sink_lo, sink_hi = sinks_ref[2 * g] * LOG2_E, sinks_ref[2 * g + 1] * LOG2_E
        m_lo = jnp.maximum(jnp.max(jnp.maximum(s_lo, jnp.where(meta_lo, sm, neg_inf)),
                                   axis=-1, keepdims=True), sink_lo)
        m_hi = jnp.maximum(jnp.max(jnp.maximum(s_hi, jnp.where(meta_lo | ~meta_any, neg_inf, sm)),
                                   axis=-1, keepdims=True), sink_hi)
        p_lo = jnp.exp2(s_lo - m_lo)
        p_hi = jnp.exp2(s_hi - m_hi)
        pm = jnp.where(meta_any, jnp.exp2(sm - jnp.where(meta_lo, m_lo, m_hi)), 0.0)
        pcat = jnp.concatenate([jnp.where(in_cur1, 0.0, p_lo), jnp.where(in_cur1, 0.0, p_hi),
                                jnp.where(in_cur1, p_lo, 0.0), jnp.where(in_cur1, p_hi, 0.0)],
                               axis=1).astype(BF16)
        sink_term = jnp.where(lower_lanes, jnp.exp2(sink_lo - m_lo), jnp.exp2(sink_hi - m_hi))
        probs.append((pcat, pm.astype(BF16), sink_term))
    ones_a = onesa_ref[...]
    for half in range(blocks_per_tile):
        blk, prev = a_blocks(half)
        cur0 = row_start(blk)
        prev0 = cur0 if prev is None else row_start(prev)
        oa = []
        for g in range(N_PAIRS_A):
            c = (2 * g) // (N_HEADS_A // N_KV_A)
            csl = slice(c * LANES, (c + 1) * LANES)
            pcat, pm, sink_term = probs[half * N_PAIRS_A + g]
            vcat = jnp.concatenate(
                [valo_ref[0, pl.ds(prev0, BLOCK), csl], vahi_ref[0, pl.ds(prev0, BLOCK), csl],
                 valo_ref[0, pl.ds(cur0, BLOCK), csl], vahi_ref[0, pl.ds(cur0, BLOCK), csl]], axis=0)
            res = _dot(pcat, jnp.concatenate([vcat, ones_a], axis=1)) + _dot(pm, mva_ref[c])
            oa.append(res[:, :LANES] / (res[:, LANES:] + sink_term))
        oa = jnp.concatenate(oa, axis=1)
        out_ref[0, row0 + half * BLOCK:row0 + (half + 1) * BLOCK, :WIDTH_A] = (
            _rms(oa, ga_ref[...]).astype(out_ref.dtype))

    carry_ref[...] = jnp.zeros_like(carry_ref)
    oacc_ref[...] = jnp.zeros_like(oacc_ref)
    tri = tri_ref[...]
    k_minus_q = (lax.broadcasted_iota(jnp.int32, (Q_TILE, LANES), 1)
                 - lax.broadcasted_iota(jnp.int32, (Q_TILE, LANES), 0))
    meta_valid = lax.broadcasted_iota(jnp.int32, (Q_TILE, LANES), 1) >= PAD

    def pair_lanes(g):
        return slice(g * LANES, (g + 1) * LANES)

    def pair_scores(g, rows, load_kt):
        q_rows = slice(row0 + rows.start, row0 + rows.stop)
        return _dot(qb_ref[0, q_rows, pair_lanes(g)], jnp.concatenate(load_kt(pair_lanes(g)), axis=1))

    def scores(rows, load_kt):
        return [pair_scores(g, rows, load_kt) for g in range(N_PAIRS_B)]

    def masked(t, valid):
        n = valid.shape[0]
        head = jnp.where(valid, t[:n], 0.0)
        return head if n == t.shape[0] else jnp.concatenate([head, t[n:]], axis=0)

    def b_step(zs, load_v, valid, rows=slice(0, Q_TILE), carry_max_rows=None, ahead=None):
        rs, row_sums = [], []
        valid2 = None if valid is None else jnp.concatenate([valid, valid], axis=1)
        for g in range(N_PAIRS_B):
            z = zs[g]
            sp = jnp.maximum(z, 0.0) + jnp.log2(1.0 + jnp.exp2(-jnp.abs(z)))
            if valid2 is not None:
                sp = masked(sp, valid2)
            rs.append(_dot(sp.astype(BF16), tri))
            row_sums.append([jnp.sum(sp[:, h * BLOCK:(h + 1) * BLOCK], axis=-1, keepdims=True)
                             for h in range(2)])
            if ahead is not None:
                slot, load_kt, ahead_rows = ahead
                zbuf_ref[slot, g, ahead_rows] = pair_scores(g, ahead_rows, load_kt)
        carry_max = None
        for g in range(N_PAIRS_B):
            a_heads = []
            for h in range(2):
                keys = slice(h * BLOCK, (h + 1) * BLOCK)
                carry = carry_ref[g, h, rows]
                a = jnp.exp2(zs[g][:, keys] + rs[g][:, keys] + carry)
                if valid is not None:
                    a = masked(a, valid)
                carry = carry - row_sums[g][h]
                carry_ref[g, h, rows] = carry
                if carry_max_rows is not None:
                    watched = carry[carry_max_rows]
                    carry_max = watched if carry_max is None else jnp.maximum(carry_max, watched)
                a_heads.append(a.astype(BF16))
            oacc_ref[rows, pair_lanes(g)] += _dot(jnp.concatenate(a_heads, axis=1),
                                                  jnp.concatenate(load_v(pair_lanes(g)), axis=0))
        return None if carry_max is None else jnp.max(carry_max)

    def real_kt(kb):
        return lambda sl: (ktlo_ref[0, kb, sl, :], kthi_ref[0, kb, sl, :])

    def real_v(kb):
        r0 = row_start(kb)
        return lambda sl: (vblo_ref[0, pl.ds(r0, BLOCK), sl], vbhi_ref[0, pl.ds(r0, BLOCK), sl])

    def stashed(slot, rows):
        return [zbuf_ref[slot, g, rows] for g in range(N_PAIRS_B)]

    all_rows = slice(0, Q_TILE)
    upper_rows = slice(BLOCK, Q_TILE)
    n_full = qi * blocks_per_tile
    top = n_full + 1
    meta_kt = lambda sl: (mktlo_ref[0, sl, :], mkthi_ref[0, sl, :])
    meta_v = lambda sl: (mvblo_ref[:, sl], mvbhi_ref[:, sl])
    lower_rows = slice(0, BLOCK)
    b_step(scores(upper_rows, real_kt(top)), real_v(top), k_minus_q[upper_rows] < -BLOCK,
           rows=upper_rows, ahead=(0, real_kt(top - 1), all_rows))
    young = slice(0, YOUNG_ROWS)
    middle = slice(YOUNG_ROWS, BLOCK + YOUNG_ROWS)
    old = slice(BLOCK + YOUNG_ROWS, Q_TILE)
    young_middle = slice(0, BLOCK + YOUNG_ROWS)
    every_row = slice(None)
    old_max = b_step(stashed(0, all_rows), real_v(top - 1), k_minus_q[lower_rows] < 0,
                     carry_max_rows=None if first_tile else old,
                     ahead=(1, meta_kt, all_rows) if first_tile else (1, real_kt(n_full - 1), young_middle))

    if first_tile:
        b_step(stashed(1, all_rows), meta_v, meta_valid)
    else:
        middle_max = b_step(stashed(1, young_middle), real_v(n_full - 1), None, rows=young_middle,
                            carry_max_rows=slice(YOUNG_ROWS, BLOCK + YOUNG_ROWS),
                            ahead=(0, real_kt(n_full - 2), young))
        young_max = b_step(stashed(0, young), real_v(n_full - 2), None, rows=young,
                           carry_max_rows=every_row)

        def alive(*carry_maxes):
            return (functools.reduce(jnp.maximum, carry_maxes) >= DEAD_LOG2).astype(jnp.int32)

        def tail_cond(state):
            j, live = state
            return jnp.logical_and(j < n_full, live > 0)

        def tail_body(state):
            j, _ = state
            kb_old = n_full - 1 - j
            maxes = []
            for lag, group in enumerate((old, middle, young)):
                kb = jnp.maximum(kb_old - lag, 0)
                exists = None if lag == 0 else (
                    k_minus_q[group] < jnp.where(kb_old >= lag, Q_TILE, -Q_TILE))
                maxes.append(b_step(scores(group, real_kt(kb)), real_v(kb), exists, rows=group,
                                    carry_max_rows=every_row))
            return j + 1, alive(*maxes)

        _, live = lax.while_loop(tail_cond, tail_body, (jnp.int32(0), alive(old_max, middle_max, young_max)))

        @pl.when(live > 0)
        def _():
            b_step(scores(all_rows, meta_kt), meta_v, meta_valid)

    out_ref[0, row0:row0 + Q_TILE, WIDTH_A:] = _rms(oacc_ref[...], gb_ref[...]).astype(out_ref.dtype)


def _attention(sinks, real, meta, gain_a, gain_b, tri, batch, seq):
    nq = seq // Q_TILE
    nkb = seq // BLOCK
    qa, katlo, kathi, valo, vahi, qb, ktlo, kthi, vblo, vbhi = real
    rows3 = lambda t: t.reshape(batch, seq, t.shape[-1])
    qa, valo, vahi, qb, vblo, vbhi = [rows3(t) for t in (qa, valo, vahi, qb, vblo, vbhi)]
    slabs = lambda t: t.reshape(batch, nkb, t.shape[-2], BLOCK)
    katlo, kathi, ktlo, kthi = [slabs(t) for t in (katlo, kathi, ktlo, kthi)]
    _, mkatlo, mkathi, mvalo, mvahi, _, mktlo, mkthi, mvblo, mvbhi = meta
    pairs = range(KV_DUP_A // LANES)
    pair = lambda c: slice(c * LANES, (c + 1) * LANES)
    mwa = jnp.stack([jnp.pad(jnp.concatenate([mkatlo[0, pair(c), PAD:], mkathi[0, pair(c), PAD:]], axis=1),
                             ((0, 0), (0, LANES - 2 * N_META))) for c in pairs])
    mva = jnp.stack([jnp.pad(jnp.concatenate([mvalo[PAD:, pair(c)], mvahi[PAD:, pair(c)]], axis=0),
                             ((0, LANES - 2 * N_META), (0, 0))) for c in pairs])
    lower_lane = jnp.arange(LANES)[None, :] < HEAD_DIM
    head_ones = lambda upper_row: (lower_lane != upper_row[:, None]).astype(BF16)
    ones_a = head_ones((jnp.arange(4 * BLOCK) // BLOCK) % 2 == 1)
    meta_row = jnp.arange(LANES)
    meta_ones = jnp.where((meta_row < 2 * N_META)[:, None], head_ones(meta_row >= N_META), 0).astype(BF16)
    mva = jnp.concatenate([mva, jnp.broadcast_to(meta_ones, mva.shape)], axis=2)
    step_rows = TILES_PER_STEP * Q_TILE
    qspec = lambda w: pl.BlockSpec((1, step_rows, w), lambda b, i: (b, i, 0))
    seqspec = lambda w: pl.BlockSpec((1, seq, w), lambda b, i: (b, 0, 0))
    ktspec = lambda w: pl.BlockSpec((1, nkb, w, BLOCK), lambda b, i: (b, 0, 0, 0))
    const = lambda a: pl.BlockSpec(a.shape, lambda b, i: (0,) * a.ndim)
    mix = WIDTH_A + WIDTH_B
    return pl.pallas_call(
        functools.partial(_attention_kernel, nq // TILES_PER_STEP),
        grid=(batch, nq // TILES_PER_STEP),
        in_specs=[pl.BlockSpec(memory_space=pltpu.SMEM),
                  qspec(WIDTH_A), ktspec(KV_DUP_A), ktspec(KV_DUP_A), seqspec(KV_DUP_A), seqspec(KV_DUP_A),
                  qspec(WIDTH_B), ktspec(WIDTH_B), ktspec(WIDTH_B), seqspec(WIDTH_B), seqspec(WIDTH_B),
                  const(mwa), const(mva), const(ones_a), const(mktlo), const(mkthi),
                  const(mvblo), const(mvbhi), const(gain_a), const(gain_b), const(tri)],
        out_specs=qspec(mix),
        out_shape=jax.ShapeDtypeStruct((batch, seq, mix), BF16),
        scratch_shapes=[pltpu.VMEM((N_PAIRS_B, 2, Q_TILE, LANES), F32),
                        pltpu.VMEM((Q_TILE, WIDTH_B), F32),
                        pltpu.VMEM((2, N_PAIRS_B, Q_TILE, 2 * BLOCK), F32)],
        compiler_params=pltpu.CompilerParams(
            dimension_semantics=("arbitrary", "arbitrary"), vmem_limit_bytes=VMEM_LIMIT),
        name="attention",
    )(sinks, qa, katlo, kathi, valo, vahi, qb, ktlo, kthi, vblo, vbhi, mwa, mva, ones_a, mktlo, mkthi,
      mvblo, mvbhi, gain_a, gain_b, tri)


def _out_ffn_kernel(ff_chunks, n_sub, mixed_ref, x_ref, wo_ref, gf_ref, wg_ref, wu_ref, wd_ref, gl_ref,
                    out_ref):
    sub = x_ref.shape[0] // n_sub
    rows = [slice(i * sub, (i + 1) * sub) for i in range(n_sub)]
    hs = [x_ref[r, :] + _dot(mixed_ref[r, :], wo_ref[...]) for r in rows]
    for r, h in zip(rows, hs):
        u = _rms(h, gf_ref[...]).astype(BF16)
        y = h
        for c0, c1 in ff_chunks:
            gate = _dot(u, wg_ref[:, c0:c1])
            up = _dot(u, wu_ref[:, c0:c1])
            act = (gate * jax.nn.sigmoid(gate) * up).astype(BF16)
            y = y + _dot(act, wd_ref[c0:c1, :])
        out_ref[r, :] = _rms(y, gl_ref[...])


def _ff_chunks(d_ff, n_chunks, col_tile):
    tiles = -(-d_ff // col_tile)
    bounds = [min(d_ff, (tiles * k // n_chunks) * col_tile) for k in range(n_chunks + 1)]
    return tuple((bounds[k], bounds[k + 1]) for k in range(n_chunks))


def _out_ffn(mixed2d, x2d, w_out, g_ffn, w_gate, w_up, w_down, g_final, row_tile):
    n, d = x2d.shape
    d_ff = w_gate.shape[1]
    row_spec = lambda w: pl.BlockSpec((row_tile, w), lambda i: (i, 0))
    const = lambda a: pl.BlockSpec(a.shape, lambda i: (0, 0), pipeline_mode=pl.Buffered(1))
    return pl.pallas_call(
        functools.partial(_out_ffn_kernel, _ff_chunks(d_ff, 2, V7X_MXU_COLS), FFN_SUB_TILES),
        grid=(n // row_tile,),
        in_specs=[row_spec(mixed2d.shape[1]), row_spec(d), const(w_out), const(g_ffn),
                  const(w_gate), const(w_up), const(w_down), const(g_final)],
        out_specs=row_spec(d),
        out_shape=jax.ShapeDtypeStruct((n, d), x2d.dtype),
        compiler_params=pltpu.CompilerParams(
            dimension_semantics=("arbitrary",), vmem_limit_bytes=VMEM_LIMIT),
        name="out_ffn",
    )(mixed2d, x2d, w_out, g_ffn, w_gate, w_up, w_down, g_final)


def _rope_tables(pos):
    half = HEAD_DIM // 2
    inv_freq = ROPE_THETA ** (-jnp.arange(half, dtype=F32) / half)
    ang = pos.astype(F32)[:, None] * inv_freq[None, :]
    cos, sin = jnp.cos(ang), jnp.sin(ang)
    reps = LANES // HEAD_DIM
    return (jnp.tile(jnp.concatenate([cos, cos], axis=1), (1, reps)),
            jnp.tile(jnp.concatenate([-sin, sin], axis=1), (1, reps)), cos.T, sin.T)


def _rearranged_w_in(w):
    o_ka = WIDTH_A
    o_va = o_ka + KV_WIDTH_A
    o_qb = o_va + KV_WIDTH_A
    o_kb = o_qb + WIDTH_B
    o_vb = o_kb + WIDTH_B
    dup = lambda t: jnp.repeat(t.reshape(t.shape[0], N_KV_A, 1, HEAD_DIM), 2, axis=2).reshape(t.shape[0], KV_DUP_A)
    w_cat = jnp.concatenate([w[:, :o_ka], dup(w[:, o_va:o_qb]), w[:, o_qb:o_kb], w[:, o_vb:]],
                            axis=1).astype(BF16)
    w_kt = jnp.concatenate([w[:, o_ka:o_va], w[:, o_kb:o_vb]], axis=1).T.astype(BF16)
    return w_cat, w_kt


def kernel(x, meta_tokens, norm_mix, w_in, sinks, norm_out_a, norm_out_b, w_out,
           norm_ffn, w_gate, w_up, w_down, norm_final):
    batch, seq, d = x.shape
    assert norm_mix.shape[0] == 1, "single-layer block: meta rows are not carried past the mixer"
    assert seq % ROW_TILE_PROJ == 0 and (batch * seq) % ROW_TILE_FFN == 0 and seq % (TILES_PER_STEP * Q_TILE) == 0
    x2d = x.reshape(batch * seq, d)
    w_cat, w_kt = _rearranged_w_in(w_in[0])
    assert w_cat.shape[1] == _C_END
    g_mix = norm_mix[0].reshape(1, d)

    meta_block = jnp.concatenate([jnp.zeros((PAD, d), x.dtype), meta_tokens.astype(x.dtype)], axis=0)
    real = _in_proj(x2d, g_mix, w_cat, w_kt, _rope_tables(jnp.arange(seq) + N_META), ROW_TILE_PROJ)
    meta = _in_proj(meta_block, g_mix, w_cat, w_kt, _rope_tables(jnp.arange(BLOCK) - PAD), BLOCK)

    j = jnp.arange(2 * BLOCK)[:, None]
    s = jnp.arange(2 * BLOCK)[None, :]
    tri = jnp.where((j // BLOCK == s // BLOCK) & (j >= s), -1.0, 0.0).astype(BF16)

    mixed = _attention(sinks[0], real, meta, norm_out_a[0].reshape(1, WIDTH_A),
                       norm_out_b[0].reshape(1, WIDTH_B), tri, batch, seq)

    out = _out_ffn(mixed.reshape(batch * seq, WIDTH_A + WIDTH_B), x2d, w_out[0].astype(BF16),
                   norm_ffn[0].reshape(1, d), w_gate[0].astype(BF16), w_up[0].astype(BF16),
                   w_down[0].astype(BF16), norm_final.reshape(1, d), ROW_TILE_FFN)
    return out.reshape(batch, seq, d)
```

```python
import functools

import jax
import jax.numpy as jnp
from jax import lax
from jax.experimental import pallas as pl
from jax.experimental.pallas import tpu as pltpu

HEAD_DIM = 64
N_HEADS_A = 8
N_KV_A = 2
N_HEADS_B = 8
WIDTH_A = N_HEADS_A * HEAD_DIM
KV_WIDTH_A = N_KV_A * HEAD_DIM
WIDTH_B = N_HEADS_B * HEAD_DIM
BLOCK = 128
N_META = 16
PAD = BLOCK - N_META
ROPE_THETA = 10000.0
EPS = 1e-6
LANES = 128
V7X_MXU_COLS = 256
V7X_VMEM_BYTES = 64 * 1024 * 1024
N_PAIRS_A = WIDTH_A // LANES
N_PAIRS_B = WIDTH_B // LANES
KV_DUP_A = 2 * KV_WIDTH_A

Q_TILE = 2 * BLOCK
TILES_PER_STEP = 8
YOUNG_ROWS = 32
ROW_TILE_PROJ = 1024
PROJ_SUB_ROWS = 256
ROW_TILE_FFN = 1024
FFN_SUB_TILES = 4
VMEM_LIMIT = V7X_VMEM_BYTES * 7 // 8

F32 = jnp.float32
BF16 = jnp.bfloat16
LOG2_E = 1.4426950408889634
DEAD_LOG2 = -150.0

_C_QA = 0
_C_VA = _C_QA + WIDTH_A
_C_QB = _C_VA + KV_DUP_A
_C_VB = _C_QB + WIDTH_B
_C_END = _C_VB + WIDTH_B


def _rms(xf, g):
    ms = jnp.mean(xf * xf, axis=-1, keepdims=True)
    return xf * lax.rsqrt(ms + EPS) * g


def _dot(a, b):
    return jnp.dot(a, b, preferred_element_type=F32)


def _dot_nt(a, b):
    return lax.dot_general(a, b, (((1,), (1,)), ((), ())), preferred_element_type=F32)


def _in_proj_kernel(sub_rows, *refs):
    for r0 in range(0, refs[0].shape[0], sub_rows):
        _in_proj_rows(slice(r0, r0 + sub_rows), r0 // BLOCK, *refs)


def _in_proj_rows(rows, kb0, x_ref, g_ref, w_ref, wkt_ref, cos_ref, sin_ref, cosf_ref, sinf_ref,
                  qa_ref, katlo_ref, kathi_ref, valo_ref, vahi_ref,
                  qb_ref, ktlo_ref, kthi_ref, vblo_ref, vbhi_ref):
    tm = rows.stop - rows.start
    u = _rms(x_ref[rows, :], g_ref[...]).astype(BF16)
    proj = _dot(u, w_ref[...])
    cos = cos_ref[rows, :]
    sin = sin_ref[rows, :]
    lane = lax.broadcasted_iota(jnp.int32, (tm, LANES), 1)
    first_half = (lane & (HEAD_DIM // 2)) == 0
    lower_head = lane < HEAD_DIM
    scale = HEAD_DIM ** -0.5

    def rope(t):
        rot = jnp.where(first_half,
                        pltpu.roll(t, LANES - HEAD_DIM // 2, 1),
                        pltpu.roll(t, HEAD_DIM // 2, 1))
        return t * cos + rot * sin

    def group(c0, j):
        return proj[:, c0 + j * LANES:c0 + (j + 1) * LANES]

    for j in range(N_PAIRS_A):
        sl = slice(j * LANES, (j + 1) * LANES)
        qa_ref[rows, sl] = (rope(group(_C_QA, j)) * (scale * LOG2_E)).astype(BF16)
    for j in range(KV_DUP_A // LANES):
        sl = slice(j * LANES, (j + 1) * LANES)
        v = group(_C_VA, j)
        valo_ref[rows, sl] = jnp.where(lower_head, v, 0.0).astype(BF16)
        vahi_ref[rows, sl] = jnp.where(lower_head, 0.0, v).astype(BF16)
    for j in range(N_PAIRS_B):
        sl = slice(j * LANES, (j + 1) * LANES)
        qb_ref[rows, sl] = (group(_C_QB, j) * (scale * LOG2_E)).astype(BF16)
        v = group(_C_VB, j)
        vblo_ref[rows, sl] = jnp.where(lower_head, v, 0.0).astype(BF16)
        vbhi_ref[rows, sl] = jnp.where(lower_head, 0.0, v).astype(BF16)

    kt_all = _dot_nt(wkt_ref[...], u)

    cos_f = cosf_ref[:, rows]
    sin_f = sinf_ref[:, rows]
    half = HEAD_DIM // 2
    zeros = jnp.zeros((HEAD_DIM, tm), F32)
    lo_parts, hi_parts = [], []
    for c in range(N_KV_A):
        x1 = kt_all[c * HEAD_DIM:c * HEAD_DIM + half]
        x2 = kt_all[c * HEAD_DIM + half:(c + 1) * HEAD_DIM]
        roped = jnp.concatenate([x1 * cos_f - x2 * sin_f, x2 * cos_f + x1 * sin_f], axis=0)
        lo_parts += [roped, zeros]
        hi_parts += [zeros, roped]
    kat_lo = jnp.concatenate(lo_parts, axis=0).astype(BF16)
    kat_hi = jnp.concatenate(hi_parts, axis=0).astype(BF16)

    kt = kt_all[KV_WIDTH_A:]
    lower_feat = (lax.broadcasted_iota(jnp.int32, kt.shape, 0) & HEAD_DIM) == 0
    kt_lo = jnp.where(lower_feat, kt, 0.0).astype(BF16)
    kt_hi = jnp.where(lower_feat, 0.0, kt).astype(BF16)
    for j in range(tm // BLOCK):
        cols = slice(j * BLOCK, (j + 1) * BLOCK)
        katlo_ref[kb0 + j] = kat_lo[:, cols]
        kathi_ref[kb0 + j] = kat_hi[:, cols]
        ktlo_ref[kb0 + j] = kt_lo[:, cols]
        kthi_ref[kb0 + j] = kt_hi[:, cols]


def _in_proj(x2d, gain, w_cat, w_kt, tables, row_tile):
    n, d = x2d.shape
    cos_t, sin_t, cos_f, sin_f = tables
    table_tiles = cos_t.shape[0] // row_tile
    kb_per_tile = row_tile // BLOCK
    row_spec = lambda w: pl.BlockSpec((row_tile, w), lambda i: (i, 0))
    const_spec = lambda shape: pl.BlockSpec(shape, lambda i: (0, 0), pipeline_mode=pl.Buffered(1))
    table_spec = pl.BlockSpec((row_tile, LANES), lambda i: (i % table_tiles, 0))
    ftable_spec = pl.BlockSpec((HEAD_DIM // 2, row_tile), lambda i: (0, i % table_tiles))
    row_out = lambda w: (row_spec(w), jax.ShapeDtypeStruct((n, w), BF16))
    kt_out = lambda w: (pl.BlockSpec((kb_per_tile, w, BLOCK), lambda i: (i, 0, 0)),
                        jax.ShapeDtypeStruct((n // BLOCK, w, BLOCK), BF16))
    outs = [row_out(WIDTH_A), kt_out(KV_DUP_A), kt_out(KV_DUP_A), row_out(KV_DUP_A), row_out(KV_DUP_A),
            row_out(WIDTH_B), kt_out(WIDTH_B), kt_out(WIDTH_B), row_out(WIDTH_B), row_out(WIDTH_B)]
    return pl.pallas_call(
        functools.partial(_in_proj_kernel, min(row_tile, PROJ_SUB_ROWS)),
        grid=(n // row_tile,),
        in_specs=[row_spec(d), const_spec((1, d)), const_spec(w_cat.shape), const_spec(w_kt.shape),
                  table_spec, table_spec, ftable_spec, ftable_spec],
        out_specs=[o[0] for o in outs],
        out_shape=[o[1] for o in outs],
        compiler_params=pltpu.CompilerParams(
            dimension_semantics=("arbitrary",), vmem_limit_bytes=VMEM_LIMIT),
        name="in_proj",
    )(x2d, gain, w_cat, w_kt, cos_t, sin_t, cos_f, sin_f)


def _attention_kernel(steps_per_seq, *refs):
    def first_step():
        for t in range(TILES_PER_STEP):
            _attention_tile(t, t * Q_TILE, *refs)

    if steps_per_seq == 1:
        first_step()
        return
    step = pl.program_id(1)
    pl.when(step == 0)(first_step)

    @pl.when(step > 0)
    def _():
        for t in range(TILES_PER_STEP):
            _attention_tile(step * TILES_PER_STEP + t, t * Q_TILE, *refs)


def _attention_tile(qi, row0, sinks_ref, qa_ref, katlo_ref, kathi_ref, valo_ref, vahi_ref,
                    qb_ref, ktlo_ref, kthi_ref, vblo_ref, vbhi_ref,
                    mwa_ref, mva_ref, onesa_ref, mktlo_ref, mkthi_ref, mvblo_ref, mvbhi_ref,
                    ga_ref, gb_ref, tri_ref, out_ref, carry_ref, oacc_ref, zbuf_ref):
    first_tile = isinstance(qi, int) and qi == 0
    blocks_per_tile = Q_TILE // BLOCK
    assert blocks_per_tile == 2, "the group-B sweep peels exactly two diagonal key blocks"

    def row_start(blk):
        return blk * BLOCK if isinstance(blk, int) else pl.multiple_of(blk * BLOCK, BLOCK)

    in_cur = ((lax.broadcasted_iota(jnp.int32, (BLOCK, 2 * BLOCK), 1) & (BLOCK - 1))
              <= lax.broadcasted_iota(jnp.int32, (BLOCK, 2 * BLOCK), 0))
    in_cur1 = in_cur[:, :BLOCK]
    lane = lax.broadcasted_iota(jnp.int32, (BLOCK, LANES), 1)
    meta_lo = lane < N_META
    meta_any = lane < 2 * N_META
    lower_lanes = lane < HEAD_DIM
    neg_inf = -jnp.inf
    chains = [(half, g) for half in range(blocks_per_tile) for g in range(N_PAIRS_A)]

    def a_blocks(half):
        blk = qi * blocks_per_tile + half
        return blk, (None if first_tile and half == 0 else blk - 1)

    def a_keys(blk, csl):
        return jnp.concatenate([katlo_ref[0, blk, csl, :], kathi_ref[0, blk, csl, :]], axis=1)

    scores = []
    for half, g in chains:
        blk, prev = a_blocks(half)
        c = (2 * g) // (N_HEADS_A // N_KV_A)
        csl = slice(c * LANES, (c + 1) * LANES)
        q2 = qa_ref[0, row0 + half * BLOCK:row0 + (half + 1) * BLOCK, g * LANES:(g + 1) * LANES]
        s_prev = neg_inf if prev is None else _dot(q2, a_keys(prev, csl))
        scores.append((_dot(q2, a_keys(blk, csl)), s_prev, _dot(q2, mwa_ref[c])))
    top_early = qi * blocks_per_tile + 1
    z_top_early = [_dot(qb_ref[0, row0 + BLOCK:row0 + Q_TILE, g * LANES:(g + 1) * LANES],
                        jnp.concatenate([ktlo_ref[0, top_early, g * LANES:(g + 1) * LANES, :],
                                         kthi_ref[0, top_early, g * LANES:(g + 1) * LANES, :]], axis=1))
                   for g in range(N_PAIRS_B)]
    probs = []
    for (half, g), (s_cur, s_prev, sm) in zip(chains, scores):
        s = jnp.where(in_cur, s_cur, s_prev)
        s_lo, s_hi = s[:, :BLOCK], s[:, BLOCK:]
        sink_lo, sink_hi = sinks_ref[2 * g] * LOG2_E, sinks_ref[2 * g + 1] * LOG2_E
        m_lo = jnp.maximum(jnp.max(jnp.maximum(s_lo, jnp.where(meta_lo, sm, neg_inf)),
                                   axis=-1, keepdims=True), sink_lo)
        m_hi = jnp.maximum(jnp.max(jnp.maximum(s_hi, jnp.where(meta_lo | ~meta_any, neg_inf, sm)),
                                   axis=-1, keepdims=True), sink_hi)
        p_lo = jnp.exp2(s_lo - m_lo)
        p_hi = jnp.exp2(s_hi - m_hi)
        pm = jnp.where(meta_any, jnp.exp2(sm - jnp.where(meta_lo, m_lo, m_hi)), 0.0)
        pcat = jnp.concatenate([jnp.where(in_cur1, 0.0, p_lo), jnp.where(in_cur1, 0.0, p_hi),
                                jnp.where(in_cur1, p_lo, 0.0), jnp.where(in_cur1, p_hi, 0.0)],
                               axis=1).astype(BF16)
        sink_term = jnp.where(lower_lanes, jnp.exp2(sink_lo - m_lo), jnp.exp2(sink_hi - m_hi))
        probs.append((pcat, pm.astype(BF16), sink_term))
    ones_a = onesa_ref[...]
    for half in range(blocks_per_tile):
        blk, prev = a_blocks(half)
        cur0 = row_start(blk)
        prev0 = cur0 if prev is None else row_start(prev)
        oa = []
        for g in range(N_PAIRS_A):
            c = (2 * g) // (N_HEADS_A // N_KV_A)
            csl = slice(c * LANES, (c + 1) * LANES)
            pcat, pm, sink_term = probs[half * N_PAIRS_A + g]
            vcat = jnp.concatenate(
                [valo_ref[0, pl.ds(prev0, BLOCK), csl], vahi_ref[0, pl.ds(prev0, BLOCK), csl],
                 valo_ref[0, pl.ds(cur0, BLOCK), csl], vahi_ref[0, pl.ds(cur0, BLOCK), csl]], axis=0)
            res = _dot(pcat, jnp.concatenate([vcat, ones_a], axis=1)) + _dot(pm, mva_ref[c])
            oa.append(res[:, :LANES] / (res[:, LANES:] + sink_term))
        oa = jnp.concatenate(oa, axis=1)
        out_ref[0, row0 + half * BLOCK:row0 + (half + 1) * BLOCK, :WIDTH_A] = (
            _rms(oa, ga_ref[...]).astype(out_ref.dtype))

    carry_ref[...] = jnp.zeros_like(carry_ref)
    oacc_ref[...] = jnp.zeros_like(oacc_ref)
    tri = tri_ref[...]
    k_minus_q = (lax.broadcasted_iota(jnp.int32, (Q_TILE, LANES), 1)
                 - lax.broadcasted_iota(jnp.int32, (Q_TILE, LANES), 0))
    meta_valid = lax.broadcasted_iota(jnp.int32, (Q_TILE, LANES), 1) >= PAD

    def pair_lanes(g):
        return slice(g * LANES, (g + 1) * LANES)

    def pair_scores(g, rows, load_kt):
        q_rows = slice(row0 + rows.start, row0 + rows.stop)
        return _dot(qb_ref[0, q_rows, pair_lanes(g)], jnp.concatenate(load_kt(pair_lanes(g)), axis=1))

    def scores(rows, load_kt):
        return [pair_scores(g, rows, load_kt) for g in range(N_PAIRS_B)]

    def masked(t, valid):
        n = valid.shape[0]
        head = jnp.where(valid, t[:n], 0.0)
        return head if n == t.shape[0] else jnp.concatenate([head, t[n:]], axis=0)

    def b_step(zs, load_v, valid, rows=slice(0, Q_TILE), carry_max_rows=None, ahead=None):
        rs, row_sums = [], []
        valid2 = None if valid is None else jnp.concatenate([valid, valid], axis=1)
        for g in range(N_PAIRS_B):
            z = zs[g]
            sp = jnp.maximum(z, 0.0) + jnp.log2(1.0 + jnp.exp2(-jnp.abs(z)))
            if valid2 is not None:
                sp = masked(sp, valid2)
            rs.append(_dot(sp.astype(BF16), tri))
            row_sums.append([jnp.sum(sp[:, h * BLOCK:(h + 1) * BLOCK], axis=-1, keepdims=True)
                             for h in range(2)])
            if ahead is not None:
                slot, load_kt, ahead_rows = ahead
                zbuf_ref[slot, g, ahead_rows] = pair_scores(g, ahead_rows, load_kt)
        carry_max = None
        for g in range(N_PAIRS_B):
            a_heads = []
            for h in range(2):
                keys = slice(h * BLOCK, (h + 1) * BLOCK)
                carry = carry_ref[g, h, rows]
                a = jnp.exp2(zs[g][:, keys] + rs[g][:, keys] + carry)
                if valid is not None:
                    a = masked(a, valid)
                carry = carry - row_sums[g][h]
                carry_ref[g, h, rows] = carry
                if carry_max_rows is not None:
                    watched = carry[carry_max_rows]
                    carry_max = watched if carry_max is None else jnp.maximum(carry_max, watched)
                a_heads.append(a.astype(BF16))
            oacc_ref[rows, pair_lanes(g)] += _dot(jnp.concatenate(a_heads, axis=1),
                                                  jnp.concatenate(load_v(pair_lanes(g)), axis=0))
        return None if carry_max is None else jnp.max(carry_max)

    def real_kt(kb):
        return lambda sl: (ktlo_ref[0, kb, sl, :], kthi_ref[0, kb, sl, :])

    def real_v(kb):
        r0 = row_start(kb)
        return lambda sl: (vblo_ref[0, pl.ds(r0, BLOCK), sl], vbhi_ref[0, pl.ds(r0, BLOCK), sl])

    def stashed(slot, rows):
        return [zbuf_ref[slot, g, rows] for g in range(N_PAIRS_B)]

    all_rows = slice(0, Q_TILE)
    upper_rows = slice(BLOCK, Q_TILE)
    n_full = qi * blocks_per_tile
    top = n_full + 1
    meta_kt = lambda sl: (mktlo_ref[0, sl, :], mkthi_ref[0, sl, :])
    meta_v = lambda sl: (mvblo_ref[:, sl], mvbhi_ref[:, sl])
    lower_rows = slice(0, BLOCK)
    b_step(z_top_early, real_v(top), k_minus_q[upper_rows] < -BLOCK,
           rows=upper_rows, ahead=(0, real_kt(top - 1), all_rows))
    young = slice(0, YOUNG_ROWS)
    middle = slice(YOUNG_ROWS, BLOCK + YOUNG_ROWS)
    old = slice(BLOCK + YOUNG_ROWS, Q_TILE)
    young_middle = slice(0, BLOCK + YOUNG_ROWS)
    every_row = slice(None)
    old_max = b_step(stashed(0, all_rows), real_v(top - 1), k_minus_q[lower_rows] < 0,
                     carry_max_rows=None if first_tile else old,
                     ahead=(1, meta_kt, all_rows) if first_tile else (1, real_kt(n_full - 1), young_middle))

    if first_tile:
        b_step(stashed(1, all_rows), meta_v, meta_valid)
    else:
        middle_max = b_step(stashed(1, young_middle), real_v(n_full - 1), None, rows=young_middle,
                            carry_max_rows=slice(YOUNG_ROWS, BLOCK + YOUNG_ROWS),
                            ahead=(0, real_kt(n_full - 2), young))
        young_max = b_step(stashed(0, young), real_v(n_full - 2), None, rows=young,
                           carry_max_rows=every_row)

        def alive(*carry_maxes):
            return (functools.reduce(jnp.maximum, carry_maxes) >= DEAD_LOG2).astype(jnp.int32)

        def tail_cond(state):
            j, live = state
            return jnp.logical_and(j < n_full, live > 0)

        def tail_body(state):
            j, _ = state
            kb_old = n_full - 1 - j
            maxes = []
            for lag, group in enumerate((old, middle, young)):
                kb = jnp.maximum(kb_old - lag, 0)
                exists = None if lag == 0 else (
                    k_minus_q[group] < jnp.where(kb_old >= lag, Q_TILE, -Q_TILE))
                maxes.append(b_step(scores(group, real_kt(kb)), real_v(kb), exists, rows=group,
                                    carry_max_rows=every_row))
            return j + 1, alive(*maxes)

        _, live = lax.while_loop(tail_cond, tail_body, (jnp.int32(0), alive(old_max, middle_max, young_max)))

        @pl.when(live > 0)
        def _():
            b_step(scores(all_rows, meta_kt), meta_v, meta_valid)

    out_ref[0, row0:row0 + Q_TILE, WIDTH_A:] = _rms(oacc_ref[...], gb_ref[...]).astype(out_ref.dtype)


def _attention(sinks, real, meta, gain_a, gain_b, tri, batch, seq):
    nq = seq // Q_TILE
    nkb = seq // BLOCK
    qa, katlo, kathi, valo, vahi, qb, ktlo, kthi, vblo, vbhi = real
    rows3 = lambda t: t.reshape(batch, seq, t.shape[-1])
    qa, valo, vahi, qb, vblo, vbhi = [rows3(t) for t in (qa, valo, vahi, qb, vblo, vbhi)]
    slabs = lambda t: t.reshape(batch, nkb, t.shape[-2], BLOCK)
    katlo, kathi, ktlo, kthi = [slabs(t) for t in (katlo, kathi, ktlo, kthi)]
    _, mkatlo, mkathi, mvalo, mvahi, _, mktlo, mkthi, mvblo, mvbhi = meta
    pairs = range(KV_DUP_A // LANES)
    pair = lambda c: slice(c * LANES, (c + 1) * LANES)
    mwa = jnp.stack([jnp.pad(jnp.concatenate([mkatlo[0, pair(c), PAD:], mkathi[0, pair(c), PAD:]], axis=1),
                             ((0, 0), (0, LANES - 2 * N_META))) for c in pairs])
    mva = jnp.stack([jnp.pad(jnp.concatenate([mvalo[PAD:, pair(c)], mvahi[PAD:, pair(c)]], axis=0),
                             ((0, LANES - 2 * N_META), (0, 0))) for c in pairs])
    lower_lane = jnp.arange(LANES)[None, :] < HEAD_DIM
    head_ones = lambda upper_row: (lower_lane != upper_row[:, None]).astype(BF16)
    ones_a = head_ones((jnp.arange(4 * BLOCK) // BLOCK) % 2 == 1)
    meta_row = jnp.arange(LANES)
    meta_ones = jnp.where((meta_row < 2 * N_META)[:, None], head_ones(meta_row >= N_META), 0).astype(BF16)
    mva = jnp.concatenate([mva, jnp.broadcast_to(meta_ones, mva.shape)], axis=2)
    step_rows = TILES_PER_STEP * Q_TILE
    qspec = lambda w: pl.BlockSpec((1, step_rows, w), lambda b, i: (b, i, 0))
    seqspec = lambda w: pl.BlockSpec((1, seq, w), lambda b, i: (b, 0, 0))
    ktspec = lambda w: pl.BlockSpec((1, nkb, w, BLOCK), lambda b, i: (b, 0, 0, 0))
    const = lambda a: pl.BlockSpec(a.shape, lambda b, i: (0,) * a.ndim)
    mix = WIDTH_A + WIDTH_B
    return pl.pallas_call(
        functools.partial(_attention_kernel, nq // TILES_PER_STEP),
        grid=(batch, nq // TILES_PER_STEP),
        in_specs=[pl.BlockSpec(memory_space=pltpu.SMEM),
                  qspec(WIDTH_A), ktspec(KV_DUP_A), ktspec(KV_DUP_A), seqspec(KV_DUP_A), seqspec(KV_DUP_A),
                  qspec(WIDTH_B), ktspec(WIDTH_B), ktspec(WIDTH_B), seqspec(WIDTH_B), seqspec(WIDTH_B),
                  const(mwa), const(mva), const(ones_a), const(mktlo), const(mkthi),
                  const(mvblo), const(mvbhi), const(gain_a), const(gain_b), const(tri)],
        out_specs=qspec(mix),
        out_shape=jax.ShapeDtypeStruct((batch, seq, mix), BF16),
        scratch_shapes=[pltpu.VMEM((N_PAIRS_B, 2, Q_TILE, LANES), F32),
                        pltpu.VMEM((Q_TILE, WIDTH_B), F32),
                        pltpu.VMEM((2, N_PAIRS_B, Q_TILE, 2 * BLOCK), F32)],
        compiler_params=pltpu.CompilerParams(
            dimension_semantics=("arbitrary", "arbitrary"), vmem_limit_bytes=VMEM_LIMIT),
        name="attention",
    )(sinks, qa, katlo, kathi, valo, vahi, qb, ktlo, kthi, vblo, vbhi, mwa, mva, ones_a, mktlo, mkthi,
      mvblo, mvbhi, gain_a, gain_b, tri)


def _out_ffn_kernel(ff_chunks, n_sub, mixed_ref, x_ref, wo_ref, gf_ref, wg_ref, wu_ref, wd_ref, gl_ref,
                    out_ref):
    sub = x_ref.shape[0] // n_sub
    rows = [slice(i * sub, (i + 1) * sub) for i in range(n_sub)]
    hs = [x_ref[r, :] + _dot(mixed_ref[r, :], wo_ref[...]) for r in rows]
    for r, h in zip(rows, hs):
        u = _rms(h, gf_ref[...]).astype(BF16)
        y = h
        for c0, c1 in ff_chunks:
            gate = _dot(u, wg_ref[:, c0:c1])
            up = _dot(u, wu_ref[:, c0:c1])
            act = (gate * jax.nn.sigmoid(gate) * up).astype(BF16)
            y = y + _dot(act, wd_ref[c0:c1, :])
        out_ref[r, :] = _rms(y, gl_ref[...])


def _ff_chunks(d_ff, n_chunks, col_tile):
    tiles = -(-d_ff // col_tile)
    bounds = [min(d_ff, (tiles * k // n_chunks) * col_tile) for k in range(n_chunks + 1)]
    return tuple((bounds[k], bounds[k + 1]) for k in range(n_chunks))


def _out_ffn(mixed2d, x2d, w_out, g_ffn, w_gate, w_up, w_down, g_final, row_tile):
    n, d = x2d.shape
    d_ff = w_gate.shape[1]
    row_spec = lambda w: pl.BlockSpec((row_tile, w), lambda i: (i, 0))
    const = lambda a: pl.BlockSpec(a.shape, lambda i: (0, 0), pipeline_mode=pl.Buffered(1))
    return pl.pallas_call(
        functools.partial(_out_ffn_kernel, _ff_chunks(d_ff, 2, V7X_MXU_COLS), FFN_SUB_TILES),
        grid=(n // row_tile,),
        in_specs=[row_spec(mixed2d.shape[1]), row_spec(d), const(w_out), const(g_ffn),
                  const(w_gate), const(w_up), const(w_down), const(g_final)],
        out_specs=row_spec(d),
        out_shape=jax.ShapeDtypeStruct((n, d), x2d.dtype),
        compiler_params=pltpu.CompilerParams(
            dimension_semantics=("arbitrary",), vmem_limit_bytes=VMEM_LIMIT),
        name="out_ffn",
    )(mixed2d, x2d, w_out, g_ffn, w_gate, w_up, w_down, g_final)


def _rope_tables(pos):
    half = HEAD_DIM // 2
    inv_freq = ROPE_THETA ** (-jnp.arange(half, dtype=F32) / half)
    ang = pos.astype(F32)[:, None] * inv_freq[None, :]
    cos, sin = jnp.cos(ang), jnp.sin(ang)
    reps = LANES // HEAD_DIM
    return (jnp.tile(jnp.concatenate([cos, cos], axis=1), (1, reps)),
            jnp.tile(jnp.concatenate([-sin, sin], axis=1), (1, reps)), cos.T, sin.T)


def _rearranged_w_in(w):
    o_ka = WIDTH_A
    o_va = o_ka + KV_WIDTH_A
    o_qb = o_va + KV_WIDTH_A
    o_kb = o_qb + WIDTH_B
    o_vb = o_kb + WIDTH_B
    dup = lambda t: jnp.repeat(t.reshape(t.shape[0], N_KV_A, 1, HEAD_DIM), 2, axis=2).reshape(t.shape[0], KV_DUP_A)
    w_cat = jnp.concatenate([w[:, :o_ka], dup(w[:, o_va:o_qb]), w[:, o_qb:o_kb], w[:, o_vb:]],
                            axis=1).astype(BF16)
    w_kt = jnp.concatenate([w[:, o_ka:o_va], w[:, o_kb:o_vb]], axis=1).T.astype(BF16)
    return w_cat, w_kt


def kernel(x, meta_tokens, norm_mix, w_in, sinks, norm_out_a, norm_out_b, w_out,
           norm_ffn, w_gate, w_up, w_down, norm_final):
    batch, seq, d = x.shape
    assert norm_mix.shape[0] == 1, "single-layer block: meta rows are not carried past the mixer"
    assert seq % ROW_TILE_PROJ == 0 and (batch * seq) % ROW_TILE_FFN == 0 and seq % (TILES_PER_STEP * Q_TILE) == 0
    x2d = x.reshape(batch * seq, d)
    w_cat, w_kt = _rearranged_w_in(w_in[0])
    assert w_cat.shape[1] == _C_END
    g_mix = norm_mix[0].reshape(1, d)

    meta_block = jnp.concatenate([jnp.zeros((PAD, d), x.dtype), meta_tokens.astype(x.dtype)], axis=0)
    real = _in_proj(x2d, g_mix, w_cat, w_kt, _rope_tables(jnp.arange(seq) + N_META), ROW_TILE_PROJ)
    meta = _in_proj(meta_block, g_mix, w_cat, w_kt, _rope_tables(jnp.arange(BLOCK) - PAD), BLOCK)

    j = jnp.arange(2 * BLOCK)[:, None]
    s = jnp.arange(2 * BLOCK)[None, :]
    tri = jnp.where((j // BLOCK == s // BLOCK) & (j >= s), -1.0, 0.0).astype(BF16)

    mixed = _attention(sinks[0], real, meta, norm_out_a[0].reshape(1, WIDTH_A),
                       norm_out_b[0].reshape(1, WIDTH_B), tri, batch, seq)

    out = _out_ffn(mixed.reshape(batch * seq, WIDTH_A + WIDTH_B), x2d, w_out[0].astype(BF16),
                   norm_ffn[0].reshape(1, d), w_gate[0].astype(BF16), w_up[0].astype(BF16),
                   w_down[0].astype(BF16), norm_final.reshape(1, d), ROW_TILE_FFN)
    return out.reshape(batch, seq, d)
```
